```python
import math, functools
import jax, jax.numpy as jnp
from jax import lax
import numpy as np

D_MODEL = 2048
BATCH = 4
SEQ = 2048
DEPTH = 1
DEC_BATCH = 128
DEC_SEQ = 8
PAST_LEN = 16384
PAGE_SIZE = 128

D_MLA = D_MODEL // 2
D_CMLP = D_MODEL - D_MLA
N_HEADS = 8
D_NOPE = 128
D_ROPE = 64
D_V = D_MLA // N_HEADS
Q_LORA = D_MODEL // 4
KV_LORA = D_MODEL // 4
CHUNK = 128
N_CMLP_HEADS = 8
D_CMLP_HEAD = D_CMLP // N_CMLP_HEADS
D_MIX = N_HEADS * D_V + D_CMLP
D_IN = Q_LORA + KV_LORA + D_ROPE + 2 * D_CMLP
D_FF = -(-8 * D_MODEL // (3 * 256)) * 256
Q_BLOCK = 128
ROPE_THETA = 10000.0
RMS_EPS = 1e-6
LN_EPS = 1e-5
NEG = -1e30
ALPHA = (2.0 * DEPTH) ** 0.25
BETA = (8.0 * DEPTH) ** -0.25
SM_SCALE = 1.0 / math.sqrt(D_NOPE + D_ROPE)

kernel_name = "hymba_mla_chunkgmlp_deepnorm_adaln_step"


def _rms(x, g):
    x = x.astype(jnp.float32)
    return x * lax.rsqrt(jnp.mean(x * x, axis=-1, keepdims=True) + RMS_EPS) * g


def _ln(x, g, b):
    x = x.astype(jnp.float32)
    mu = jnp.mean(x, axis=-1, keepdims=True)
    var = jnp.mean(jnp.square(x - mu), axis=-1, keepdims=True)
    return (x - mu) * lax.rsqrt(var + LN_EPS) * g + b


def _rope_tables(pos):
    inv = ROPE_THETA ** (-jnp.arange(0, D_ROPE, 2, dtype=jnp.float32) / D_ROPE)
    ang = pos[:, None] * inv[None, :]
    return jnp.cos(ang), jnp.sin(ang)


def _rope(x, cos, sin):
    half = x.shape[-1] // 2
    x1, x2 = x[..., :half], x[..., half:]
    return jnp.concatenate([x1 * cos - x2 * sin, x1 * sin + x2 * cos], axis=-1)


def _mla_prompt(q_lat, q_rope, c_kv, k_rope):
    B, T = q_lat.shape[0], q_lat.shape[1]
    nqb = T // Q_BLOCK
    qb = q_lat.reshape(B, nqb, Q_BLOCK, N_HEADS, KV_LORA).transpose(1, 0, 2, 3, 4)
    rb = q_rope.reshape(B, nqb, Q_BLOCK, N_HEADS, D_ROPE).transpose(1, 0, 2, 3, 4)
    kpos = jnp.arange(T)

    def block(args):
        ql, qr, i = args
        qpos = i * Q_BLOCK + jnp.arange(Q_BLOCK)
        s = (jnp.einsum('bqhl,bkl->bhqk', ql, c_kv)
             + jnp.einsum('bqhr,bkr->bhqk', qr, k_rope))
        s = jnp.where(kpos[None, :] <= qpos[:, None], s, NEG)
        p = jax.nn.softmax(s, axis=-1)
        return jnp.einsum('bhqk,bkl->bqhl', p, c_kv)

    o = lax.map(block, (qb, rb, jnp.arange(nqb)))
    return o.transpose(1, 0, 2, 3, 4).reshape(B, T, N_HEADS, KV_LORA)


def _mla_sample(q_lat, q_rope, c_kv, k_rope, cache_latent, cache_krope, page_table, layer):
    S = q_lat.shape[1]
    s = (jnp.einsum('bqhl,bkl->bhqk', q_lat, c_kv)
         + jnp.einsum('bqhr,bkr->bhqk', q_rope, k_rope))
    s = jnp.where(jnp.tril(jnp.ones((S, S), dtype=bool)), s, NEG)
    m = jnp.max(s, axis=-1)
    p = jnp.exp(s - m[..., None])
    l = jnp.sum(p, axis=-1)
    acc = jnp.einsum('bhqk,bkl->bhql', p, c_kv)

    def step(carry, pages):
        m, l, acc = carry
        kv = cache_latent[layer, pages].astype(jnp.float32)
        kr = cache_krope[layer, pages].astype(jnp.float32)
        sp = (jnp.einsum('bqhl,bkl->bhqk', q_lat, kv)
              + jnp.einsum('bqhr,bkr->bhqk', q_rope, kr))
        m_new = jnp.maximum(m, jnp.max(sp, axis=-1))
        corr = jnp.exp(m - m_new)
        pp = jnp.exp(sp - m_new[..., None])
        l = l * corr + jnp.sum(pp, axis=-1)
        acc = acc * corr[..., None] + jnp.einsum('bhqk,bkl->bhql', pp, kv)
        return (m_new, l, acc), None

    (m, l, acc), _ = lax.scan(step, (m, l, acc), page_table.T)
    return (acc / l[..., None]).transpose(0, 2, 1, 3)


def _chunk_mix(u, v, w_s, b_s):
    B, T, _ = v.shape
    R = min(T, CHUNK)
    ws = jnp.where(jnp.tril(jnp.ones((R, R), dtype=bool)), w_s[:, :R, :R], 0.0)
    vc = v.reshape(B, T // R, R, N_CMLP_HEADS, D_CMLP_HEAD)
    mixed = jnp.einsum('hij,bcjhd->bcihd', ws, vc) + b_s[:, :R].T[None, None, :, :, None]
    return u * mixed.reshape(B, T, D_CMLP), v[:, T - R:]


def _layer(x, c, cos, sin, attend, p):
    B, T, _ = x.shape
    mod = jax.nn.silu(c.astype(jnp.float32)) @ p['w_ada'] + p['b_ada']
    sh1, sc1, g1, sh2, sc2, g2 = [m[:, None, :] for m in jnp.split(mod, 6, axis=-1)]
    h = x * (1.0 + sc1) + sh1
    z = h @ p['w_in']
    i0 = Q_LORA; i1 = i0 + KV_LORA; i2 = i1 + D_ROPE; i3 = i2 + D_CMLP
    cq, ckv, kr, u, v = z[..., :i0], z[..., i0:i1], z[..., i1:i2], z[..., i2:i3], z[..., i3:]
    q = (_rms(cq, p['q_norm']) @ p['w_uq']).reshape(B, T, N_HEADS, D_NOPE + D_ROPE)
    q_nope, q_rope = q[..., :D_NOPE], q[..., D_NOPE:]
    q_rope = _rope(q_rope, cos[:, None, :], sin[:, None, :]) * SM_SCALE
    q_lat = jnp.einsum('bthd,lhd->bthl', q_nope, p['w_uk']) * SM_SCALE
    c_kv = _rms(ckv, p['kv_norm'])
    k_rope = _rope(kr, cos, sin)
    o_lat = attend(q_lat, q_rope, c_kv, k_rope)
    o_att = jnp.einsum('bthl,lhd->bthd', o_lat, p['w_uv']).reshape(B, T, N_HEADS * D_V)
    u = jax.nn.gelu(u)
    v = _ln(jax.nn.gelu(v), p['v_ln_g'], p['v_ln_b'])
    o_cm, v_rows = _chunk_mix(u, v, p['w_s'], p['b_s'])
    mix = jnp.concatenate([o_att, o_cm], axis=-1) @ p['w_out']
    x = _ln(ALPHA * x + g1 * mix, p['ln1_g'], p['ln1_b'])
    h = x * (1.0 + sc2) + sh2
    f = (jax.nn.silu(h @ p['w_gate']) * (h @ p['w_up'])) @ p['w_down']
    x = _ln(ALPHA * x + g2 * f, p['ln2_g'], p['ln2_b'])
    return x, c_kv, k_rope, v_rows


def setup_inputs(seed: int = 0) -> dict:
    key = jax.random.key(seed)
    ks = jax.random.split(key, 32)
    f32 = jnp.float32
    n_pages = PAST_LEN // PAGE_SIZE
    n_used = DEC_BATCH * n_pages
    n_phys = (n_used * 5) // 4
    nrm = lambda k, shape, s: jax.random.normal(k, shape, f32) * s
    perm = jax.random.permutation(ks[0], n_phys)[:n_used]
    page_table = perm.reshape(DEC_BATCH, n_pages).astype(jnp.int32)
    L = DEPTH
    return {
        'x_prompt': nrm(ks[1], (BATCH, SEQ, D_MODEL), 1.0),
        'x_sample': nrm(ks[2], (DEC_BATCH, DEC_SEQ, D_MODEL), 1.0),
        'cache_latent': nrm(ks[3], (L, n_phys, PAGE_SIZE, KV_LORA), 1.0),
        'cache_krope': nrm(ks[4], (L, n_phys, PAGE_SIZE, D_ROPE), 1.0),
        'page_table': page_table,
        'c_prompt': nrm(ks[5], (BATCH, D_MODEL), 1.0),
        'c_sample': nrm(ks[6], (DEC_BATCH, D_MODEL), 1.0),
        'w_ada': nrm(ks[7], (L, D_MODEL, 6 * D_MODEL), 0.5 * D_MODEL ** -0.5),
        'b_ada': nrm(ks[8], (L, 6 * D_MODEL), 0.02),
        'w_in': nrm(ks[9], (L, D_MODEL, D_IN), D_MODEL ** -0.5),
        'q_norm': 1.0 + nrm(ks[10], (L, Q_LORA), 0.02),
        'w_uq': nrm(ks[11], (L, Q_LORA, N_HEADS * (D_NOPE + D_ROPE)), Q_LORA ** -0.5),
        'kv_norm': 1.0 + nrm(ks[12], (L, KV_LORA), 0.02),
        'w_uk': nrm(ks[13], (L, KV_LORA, N_HEADS, D_NOPE), KV_LORA ** -0.5),
        'w_uv': nrm(ks[14], (L, KV_LORA, N_HEADS, D_V), KV_LORA ** -0.5),
        'v_ln_g': 1.0 + nrm(ks[15], (L, D_CMLP), 0.02),
        'v_ln_b': nrm(ks[16], (L, D_CMLP), 0.02),
        'w_s': nrm(ks[17], (L, N_CMLP_HEADS, CHUNK, CHUNK), CHUNK ** -0.5),
        'b_s': 1.0 + nrm(ks[18], (L, N_CMLP_HEADS, CHUNK), 0.1),
        'w_out': nrm(ks[19], (L, D_MIX, D_MODEL), BETA * D_MIX ** -0.5),
        'ln1_g': 1.0 + nrm(ks[20], (L, D_MODEL), 0.02),
        'ln1_b': nrm(ks[21], (L, D_MODEL), 0.02),
        'w_gate': nrm(ks[22], (L, D_MODEL, D_FF), D_MODEL ** -0.5),
        'w_up': nrm(ks[23], (L, D_MODEL, D_FF), D_MODEL ** -0.5),
        'w_down': nrm(ks[24], (L, D_FF, D_MODEL), BETA * D_FF ** -0.5),
        'ln2_g': 1.0 + nrm(ks[25], (L, D_MODEL), 0.02),
        'ln2_b': nrm(ks[26], (L, D_MODEL), 0.02),
    }


def reference(x_prompt, x_sample, cache_latent, cache_krope, page_table, c_prompt, c_sample,
              w_ada, b_ada, w_in, q_norm, w_uq, kv_norm, w_uk, w_uv, v_ln_g, v_ln_b, w_s, b_s,
              w_out, ln1_g, ln1_b, w_gate, w_up, w_down, ln2_g, ln2_b):
    xp = x_prompt.astype(jnp.float32)
    xs = x_sample.astype(jnp.float32)
    T, S = xp.shape[1], xs.shape[1]
    cos_p, sin_p = _rope_tables(jnp.arange(T, dtype=jnp.float32))
    cos_s, sin_s = _rope_tables(PAST_LEN + jnp.arange(S, dtype=jnp.float32))
    lat_p, kr_p, v_p, lat_s, kr_s, v_s = [], [], [], [], [], []
    for l in range(DEPTH):
        p = {'w_ada': w_ada[l], 'b_ada': b_ada[l], 'w_in': w_in[l], 'q_norm': q_norm[l],
             'w_uq': w_uq[l], 'kv_norm': kv_norm[l], 'w_uk': w_uk[l], 'w_uv': w_uv[l],
             'v_ln_g': v_ln_g[l], 'v_ln_b': v_ln_b[l], 'w_s': w_s[l], 'b_s': b_s[l],
             'w_out': w_out[l], 'ln1_g': ln1_g[l], 'ln1_b': ln1_b[l], 'w_gate': w_gate[l],
             'w_up': w_up[l], 'w_down': w_down[l], 'ln2_g': ln2_g[l], 'ln2_b': ln2_b[l]}
        xp, a, b, c = _layer(xp, c_prompt, cos_p, sin_p, _mla_prompt, p)
        lat_p.append(a); kr_p.append(b); v_p.append(c)
        attend_s = functools.partial(_mla_sample, cache_latent=cache_latent,
                                     cache_krope=cache_krope, page_table=page_table, layer=l)
        xs, a, b, c = _layer(xs, c_sample, cos_s, sin_s, attend_s, p)
        lat_s.append(a); kr_s.append(b); v_s.append(c)
    cdt = cache_latent.dtype
    return (xp.astype(x_prompt.dtype), xs.astype(x_sample.dtype),
            jnp.stack(lat_p).astype(cdt), jnp.stack(kr_p).astype(cdt), jnp.stack(v_p).astype(cdt),
            jnp.stack(lat_s).astype(cdt), jnp.stack(kr_s).astype(cdt), jnp.stack(v_s).astype(cdt))
```

```python
import functools
import math

import jax
import jax.numpy as jnp
from jax import lax
from jax.experimental import pallas as pl
from jax.experimental.pallas import tpu as pltpu

N_HEADS = 8
D_NOPE = 128
D_ROPE = 64
D_V = 128
N_CMLP_HEADS = 8
CHUNK = 128
PAGE_SIZE = 128
ROPE_THETA = 10000.0
RMS_EPS = 1e-6
LN_EPS = 1e-5
NEG = -1e30
SM_SCALE = 1.0 / math.sqrt(D_NOPE + D_ROPE)

LANES = 128
SUBLANES = 8
D_QK = 2 * LANES
VMEM_LIMIT = 56 * 1024 * 1024

F32 = jnp.float32
BF16 = jnp.bfloat16


def _largest_tile(n, target, multiple):
    if n <= target:
        return n
    t = (target // multiple) * multiple
    while t >= multiple:
        if n % t == 0:
            return t
        t -= multiple
    raise ValueError(f"no tile for {n}")


def _const_spec(shape):
    nd = len(shape)
    return pl.BlockSpec(shape, lambda *_: (0,) * nd, pipeline_mode=pl.Buffered(1))


def _params(semantics):
    return pltpu.CompilerParams(dimension_semantics=semantics, vmem_limit_bytes=VMEM_LIMIT)


def _layernorm(x, g, b):
    mu = jnp.mean(x, axis=-1, keepdims=True)
    xc = x - mu
    var = jnp.mean(xc * xc, axis=-1, keepdims=True)
    return xc * lax.rsqrt(var + LN_EPS) * g + b


def _rmsnorm(x, g):
    return x * lax.rsqrt(jnp.mean(x * x, axis=-1, keepdims=True) + RMS_EPS) * g


def _dot(a, b):
    return jnp.dot(a, b, preferred_element_type=F32)


def _dot_nt(a, b):
    return lax.dot_general(a, b, (((1,), (1,)), ((), ())), preferred_element_type=F32)


def _ada_kernel(c_ref, w_ref, b_ref, o_ref):
    c = c_ref[...]
    o_ref[...] = _dot(jax.nn.silu(c).astype(BF16), w_ref[...]) + b_ref[...]


def _ada(c_all, w_ada, b_ada):
    n, d = c_all.shape
    dout = w_ada.shape[1]
    tn = _largest_tile(dout, 1536, LANES)
    return pl.pallas_call(
        _ada_kernel,
        grid=(dout // tn,),
        in_specs=[
            _const_spec((n, d)),
            pl.BlockSpec((d, tn), lambda j: (0, j)),
            pl.BlockSpec((1, tn), lambda j: (0, j)),
        ],
        out_specs=pl.BlockSpec((n, tn), lambda j: (0, j)),
        out_shape=jax.ShapeDtypeStruct((n, dout), F32),
        compiler_params=_params(("arbitrary",)),
        name="ada",
    )(c_all, w_ada, b_ada.reshape(1, dout))


def _rope_tile(t, cos, sin, bb, tt):
    rolled = pltpu.roll(t, D_ROPE, axis=1)
    r = t.reshape(bb, tt, LANES) * cos[None] + rolled.reshape(bb, tt, LANES) * sin[None]
    return r.reshape(bb * tt, LANES)


def _inproj_kernel(*refs, absorbed, q_lora, kv_lora, d_cmlp):
    if absorbed:
        (x_ref, sc_ref, sh_ref, cos_ref, sin_ref, win_ref, qn_ref, kvn_ref, vg_ref, vb_ref, wq_ref, wk_ref,
         ckv_out, kr_out, u_out, v_out, q_out) = refs
    else:
        (x_ref, sc_ref, sh_ref, cos_ref, sin_ref, win_ref, qn_ref, kvn_ref, vg_ref, vb_ref, wq_ref, wk_ref, wv_ref,
         ckv_out, kr_out, u_out, v_out, q_out, k_out, vh_out) = refs
    bb, tt, d = x_ref.shape
    tm = bb * tt
    cos = cos_ref[...]
    sin = sin_ref[...]

    h = x_ref[...] * (1.0 + sc_ref[...]) + sh_ref[...]
    h = h.reshape(tm, d).astype(BF16)

    o_kv = q_lora
    o_u = o_kv + kv_lora
    o_v = o_u + d_cmlp
    o_kr = o_v + d_cmlp

    ckv = _rmsnorm(_dot(h, win_ref[:, o_kv:o_u]), kvn_ref[...])
    ckv_out[...] = ckv
    ckv_bf = ckv.astype(BF16)

    u_out[...] = jax.nn.gelu(_dot(h, win_ref[:, o_u:o_v]))
    v_out[...] = _layernorm(jax.nn.gelu(_dot(h, win_ref[:, o_v:o_kr])), vg_ref[...], vb_ref[...])

    kro = _rope_tile(_dot(h, win_ref[:, o_kr:o_kr + LANES]), cos, sin, bb, tt)
    kr_out[...] = kro[:, :D_ROPE]

    cq = _rmsnorm(_dot(h, win_ref[:, 0:q_lora]), qn_ref[...]).astype(BF16)
    n_nope = N_HEADS * D_NOPE
    q_nope = _dot(cq, wq_ref[:, 0:n_nope])
    q_rt = _dot(cq, wq_ref[:, n_nope:n_nope + N_HEADS * LANES])

    if absorbed:
        for hd in range(N_HEADS):
            qn_h = q_nope[:, hd * D_NOPE:(hd + 1) * D_NOPE].astype(BF16)
            q_lat = _dot(qn_h, wk_ref[hd]) * SM_SCALE
            r_h = _rope_tile(q_rt[:, hd * LANES:(hd + 1) * LANES], cos, sin, bb, tt) * SM_SCALE
            q_out[:, hd, :, 0:kv_lora] = q_lat.reshape(bb, tt, kv_lora)
            q_out[:, hd, :, kv_lora:kv_lora + LANES] = r_h.reshape(bb, tt, LANES)
    else:
        k_nope = _dot(ckv_bf, wk_ref[...])
        vh_out[...] = _dot(ckv_bf, wv_ref[...]).astype(BF16)
        kro_bf = kro.astype(BF16)
        for hd in range(N_HEADS):
            r_h = _rope_tile(q_rt[:, hd * LANES:(hd + 1) * LANES], cos, sin, bb, tt) * SM_SCALE
            q_out[:, hd * D_QK:hd * D_QK + D_NOPE] = (q_nope[:, hd * D_NOPE:(hd + 1) * D_NOPE] * SM_SCALE).astype(BF16)
            q_out[:, hd * D_QK + D_NOPE:(hd + 1) * D_QK] = r_h.astype(BF16)
            k_out[:, hd * D_QK:hd * D_QK + D_NOPE] = k_nope[:, hd * D_NOPE:(hd + 1) * D_NOPE].astype(BF16)
            k_out[:, hd * D_QK + D_NOPE:(hd + 1) * D_QK] = kro_bf


def _inproj(x, sc, sh, cos_t, sin_t, w, *, absorbed, tm_target):
    b, t, d = x.shape
    q_lora, kv_lora, d_cmlp = w["q_norm"].shape[-1], w["kv_norm"].shape[-1], w["v_ln_g"].shape[-1]
    if absorbed:
        tt, bb = t, _largest_tile(b, max(1, tm_target // t), 1)
    else:
        tt, bb = _largest_tile(t, tm_target, 16), 1
    nt = t // tt
    tm = bb * tt
    n_tok = b * t
    grid = (n_tok // tm,)

    row3 = lambda i: (i // nt, i % nt, 0)
    mod3 = lambda i: (i // nt, 0, 0)
    row2 = lambda i: (i, 0)
    in_specs = [
        pl.BlockSpec((bb, tt, d), row3),
        pl.BlockSpec((bb, 1, d), mod3),
        pl.BlockSpec((bb, 1, d), mod3),
        pl.BlockSpec((tt, LANES), lambda i: (i % nt, 0)),
        pl.BlockSpec((tt, LANES), lambda i: (i % nt, 0)),
        _const_spec(w["w_in"].shape),
        _const_spec((1, q_lora)),
        _const_spec((1, kv_lora)),
        _const_spec((1, d_cmlp)),
        _const_spec((1, d_cmlp)),
        _const_spec(w["w_q"].shape),
    ]
    args = [x, sc, sh, cos_t, sin_t, w["w_in"], w["q_norm"], w["kv_norm"], w["v_ln_g"], w["v_ln_b"], w["w_q"]]
    out_specs = [
        pl.BlockSpec((tm, kv_lora), row2),
        pl.BlockSpec((tm, D_ROPE), row2),
        pl.BlockSpec((tm, d_cmlp), row2),
        pl.BlockSpec((tm, d_cmlp), row2),
    ]
    out_shape = [
        jax.ShapeDtypeStruct((n_tok, kv_lora), F32),
        jax.ShapeDtypeStruct((n_tok, D_ROPE), F32),
        jax.ShapeDtypeStruct((n_tok, d_cmlp), F32),
        jax.ShapeDtypeStruct((n_tok, d_cmlp), F32),
    ]
    if absorbed:
        in_specs.append(_const_spec(w["w_uk_t"].shape))
        args.append(w["w_uk_t"])
        d_q = kv_lora + LANES
        out_specs.append(pl.BlockSpec((bb, N_HEADS, t, d_q), lambda i: (i, 0, 0, 0)))
        out_shape.append(jax.ShapeDtypeStruct((b, N_HEADS, t, d_q), F32))
    else:
        in_specs += [_const_spec(w["w_uk_f"].shape), _const_spec(w["w_uv_f"].shape)]
        args += [w["w_uk_f"], w["w_uv_f"]]
        out_specs += [
            pl.BlockSpec((tm, N_HEADS * D_QK), row2),
            pl.BlockSpec((tm, N_HEADS * D_QK), row2),
            pl.BlockSpec((tm, N_HEADS * D_V), row2),
        ]
        out_shape += [
            jax.ShapeDtypeStruct((n_tok, N_HEADS * D_QK), BF16),
            jax.ShapeDtypeStruct((n_tok, N_HEADS * D_QK), BF16),
            jax.ShapeDtypeStruct((n_tok, N_HEADS * D_V), BF16),
        ]
    kern = functools.partial(_inproj_kernel, absorbed=absorbed, q_lora=q_lora, kv_lora=kv_lora, d_cmlp=d_cmlp)
    return pl.pallas_call(
        kern, grid=grid, in_specs=in_specs, out_specs=out_specs, out_shape=out_shape,
        compiler_params=_params(("arbitrary",)),
        name="inproj_sample" if absorbed else "inproj_prompt",
    )(*args)


def _flash_kernel(q_ref, k_ref, v_ref, o_ref):
    tq = q_ref.shape[0]
    qi = pl.program_id(2)
    q = q_ref[...]
    row = lax.broadcasted_iota(jnp.int32, (tq, tq), 0)
    col = lax.broadcasted_iota(jnp.int32, (tq, tq), 1)

    def body(j, carry):
        m, l, acc = carry
        start = pl.multiple_of(j * tq, tq)
        k = k_ref[pl.ds(start, tq), :]
        v = v_ref[pl.ds(start, tq), :]
        s = _dot_nt(q, k)
        s = jnp.where((col + j * tq) <= (row + qi * tq), s, NEG)
        m_new = jnp.maximum(m, jnp.max(s, axis=-1, keepdims=True))
        corr = jnp.exp(m - m_new)
        p = jnp.exp(s - m_new)
        l = l * corr + jnp.sum(p, axis=-1, keepdims=True)
        acc = acc * corr + _dot(p.astype(BF16), v)
        return m_new, l, acc

    init = (jnp.full((tq, 1), NEG, F32), jnp.zeros((tq, 1), F32), jnp.zeros((tq, D_V), F32))
    m, l, acc = lax.fori_loop(0, qi + 1, body, init)
    o_ref[...] = (acc / l).astype(o_ref.dtype)


def _flash(q_cat, k_cat, v_h, b, t):
    tq = _largest_tile(t, 256, 16)
    nq = t // tq
    return pl.pallas_call(
        _flash_kernel,
        grid=(b, N_HEADS, nq),
        in_specs=[
            pl.BlockSpec((tq, D_QK), lambda bi, h, i: (bi * nq + i, h)),
            pl.BlockSpec((t, D_QK), lambda bi, h, i: (bi, h)),
            pl.BlockSpec((t, D_V), lambda bi, h, i: (bi, h)),
        ],
        out_specs=pl.BlockSpec((tq, D_V), lambda bi, h, i: (bi * nq + i, h)),
        out_shape=jax.ShapeDtypeStruct((b * t, N_HEADS * D_V), BF16),
        compiler_params=_params(("arbitrary", "arbitrary", "arbitrary")),
        name="flash_prompt",
    )(q_cat, k_cat, v_h)


def _paged_kernel(pt_ref, q_ref, ckv_ref, kr_ref, *refs, n_group, kv_lora):
    lat_refs = refs[:n_group]
    krp_refs = refs[n_group:2 * n_group]
    o_ref = refs[2 * n_group]
    m_ref, l_ref, acc_ref = refs[2 * n_group + 1:]
    del pt_ref
    j = pl.program_id(1)
    s_new = ckv_ref.shape[1]
    n_rows = N_HEADS * s_new

    q = q_ref[0].reshape(n_rows, kv_lora + LANES)
    q_lat = q[:, :kv_lora].astype(BF16)
    q_rope = q[:, kv_lora:kv_lora + D_ROPE].astype(BF16)

    def update(s, kv_bf):
        m_old = m_ref[...]
        m_new = jnp.maximum(m_old, jnp.max(s, axis=-1, keepdims=True))
        corr = jnp.exp(m_old - m_new)
        p = jnp.exp(s - m_new)
        l_ref[...] = l_ref[...] * corr + jnp.sum(p, axis=-1, keepdims=True)
        acc_ref[...] = acc_ref[...] * corr + _dot(p.astype(BF16), kv_bf)
        m_ref[...] = m_new

    @pl.when(j == 0)
    def _():
        pad = PAGE_SIZE - s_new
        kv = jnp.concatenate([ckv_ref[0], jnp.zeros((pad, kv_lora), F32)], axis=0).astype(BF16)
        kr = jnp.concatenate([kr_ref[0], jnp.zeros((pad, D_ROPE), F32)], axis=0).astype(BF16)
        s = _dot_nt(q_lat, kv) + _dot_nt(q_rope, kr)
        qpos = lax.broadcasted_iota(jnp.int32, (n_rows, PAGE_SIZE), 0) % s_new
        kpos = lax.broadcasted_iota(jnp.int32, (n_rows, PAGE_SIZE), 1)
        s = jnp.where(kpos <= qpos, s, NEG)
        m_ref[...] = jnp.full(m_ref.shape, NEG, F32)
        l_ref[...] = jnp.zeros(l_ref.shape, F32)
        acc_ref[...] = jnp.zeros(acc_ref.shape, F32)
        update(s, kv)

    kv = jnp.concatenate([r[0, 0] for r in lat_refs], axis=0).astype(BF16)
    kr = jnp.concatenate([r[0, 0] for r in krp_refs], axis=0).astype(BF16)
    update(_dot_nt(q_lat, kv) + _dot_nt(q_rope, kr), kv)

    @pl.when(j == pl.num_programs(1) - 1)
    def _():
        o = acc_ref[...] / l_ref[...]
        o_ref[0] = o.reshape(N_HEADS, s_new, kv_lora)


def _paged(q_cat, ckv_new, kr_new, cache_latent, cache_krope, page_table, layer):
    b, _, s_new, d_q = q_cat.shape
    kv_lora = ckv_new.shape[-1]
    n_pages = page_table.shape[1]
    n_group = _largest_tile(n_pages, 16, 1)
    n_rows = N_HEADS * s_new

    def page_spec(width, g):
        return pl.BlockSpec((1, 1, PAGE_SIZE, width), lambda bi, j, pt: (layer, pt[bi, j * n_group + g], 0, 0))

    in_specs = [
        pl.BlockSpec((1, N_HEADS, s_new, d_q), lambda bi, j, pt: (bi, 0, 0, 0)),
        pl.BlockSpec((1, s_new, kv_lora), lambda bi, j, pt: (bi, 0, 0)),
        pl.BlockSpec((1, s_new, D_ROPE), lambda bi, j, pt: (bi, 0, 0)),
    ]
    in_specs += [page_spec(kv_lora, g) for g in range(n_group)]
    in_specs += [page_spec(D_ROPE, g) for g in range(n_group)]
    grid_spec = pltpu.PrefetchScalarGridSpec(
        num_scalar_prefetch=1,
        grid=(b, n_pages // n_group),
        in_specs=in_specs,
        out_specs=pl.BlockSpec((1, N_HEADS, s_new, kv_lora), lambda bi, j, pt: (bi, 0, 0, 0)),
        scratch_shapes=[
            pltpu.VMEM((n_rows, 1), F32),
            pltpu.VMEM((n_rows, 1), F32),
            pltpu.VMEM((n_rows, kv_lora), F32),
        ],
    )
    kern = functools.partial(_paged_kernel, n_group=n_group, kv_lora=kv_lora)
    return pl.pallas_call(
        kern, grid_spec=grid_spec,
        out_shape=jax.ShapeDtypeStruct((b, N_HEADS, s_new, kv_lora), F32),
        compiler_params=_params(("arbitrary", "arbitrary")),
        name="paged_sample",
    )(page_table, q_cat, ckv_new.reshape(b, s_new, kv_lora), kr_new.reshape(b, s_new, D_ROPE),
      *([cache_latent] * n_group), *([cache_krope] * n_group))


def _mix_kernel(*refs, absorbed, alpha, run, d_att):
    if absorbed:
        (x_ref, g1_ref, sc2_ref, sh2_ref, att_ref, wuv_ref, u_ref, v_ref, ws_ref, bs_ref, wout_ref, lng_ref, lnb_ref,
         x1_out, h2_out, cat_ref) = refs
    else:
        (x_ref, g1_ref, sc2_ref, sh2_ref, att_ref, u_ref, v_ref, ws_ref, bs_ref, wout_ref, lng_ref, lnb_ref,
         x1_out, h2_out, cat_ref) = refs
    bb, tt, d = x_ref.shape
    tm = bb * tt
    mb = ws_ref.shape[1]
    d_cmlp = u_ref.shape[1]
    d_head = d_cmlp // N_CMLP_HEADS

    if absorbed:
        for hd in range(N_HEADS):
            o_lat = att_ref[:, hd].reshape(tm, att_ref.shape[-1]).astype(BF16)
            cat_ref[:, hd * D_V:(hd + 1) * D_V] = _dot(o_lat, wuv_ref[hd]).astype(BF16)
    else:
        cat_ref[:, 0:d_att] = att_ref[...]

    r = lax.broadcasted_iota(jnp.int32, (mb, mb), 0)
    c = lax.broadcasted_iota(jnp.int32, (mb, mb), 1)
    shift = run.bit_length() - 1
    keep = ((r >> shift) == (c >> shift)) & ((c & (run - 1)) <= (r & (run - 1)))
    bs = bs_ref[...]
    for hd in range(N_CMLP_HEADS):
        wm = jnp.where(keep, ws_ref[hd], 0.0).astype(BF16)
        bias = bs[:, hd:hd + 1]
        cols = slice(hd * d_head, (hd + 1) * d_head)
        for blk in range(tm // mb):
            rows = slice(blk * mb, (blk + 1) * mb)
            mixed = _dot(wm, v_ref[rows, cols].astype(BF16)) + bias
            cat_ref[rows, d_att + hd * d_head:d_att + (hd + 1) * d_head] = (u_ref[rows, cols] * mixed).astype(BF16)

    mix = _dot(cat_ref[...], wout_ref[...])
    y = alpha * x_ref[...] + g1_ref[...] * mix.reshape(bb, tt, d)
    x1 = _layernorm(y, lng_ref[...], lnb_ref[...])
    x1_out[...] = x1
    h2 = x1 * (1.0 + sc2_ref[...]) + sh2_ref[...]
    h2_out[...] = h2.reshape(tm, d).astype(BF16)


def _mix(x, g1, sc2, sh2, att, u_act, v_ln, w, *, absorbed, alpha, tm_target):
    b, t, d = x.shape
    d_cmlp = u_act.shape[1]
    d_att = N_HEADS * D_V
    run = min(t, CHUNK)
    assert run & (run - 1) == 0 and t % run == 0
    if absorbed:
        tt, bb = t, _largest_tile(b, max(1, tm_target // t), 1)
    else:
        tt, bb = _largest_tile(t, tm_target, CHUNK), 1
    nt = t // tt
    tm = bb * tt
    mb = w["ws_rep"].shape[1]
    assert tm % mb == 0
    n_tok = b * t
    row3 = lambda i: (i // nt, i % nt, 0)
    mod3 = lambda i: (i // nt, 0, 0)
    row2 = lambda i: (i, 0)
    in_specs = [
        pl.BlockSpec((bb, tt, d), row3),
        pl.BlockSpec((bb, 1, d), mod3),
        pl.BlockSpec((bb, 1, d), mod3),
        pl.BlockSpec((bb, 1, d), mod3),
    ]
    args = [x, g1, sc2, sh2]
    if absorbed:
        in_specs += [pl.BlockSpec((bb, N_HEADS, t, att.shape[-1]), lambda i: (i, 0, 0, 0)),
                     _const_spec(w["w_uv_h"].shape)]
        args += [att, w["w_uv_h"]]
    else:
        in_specs.append(pl.BlockSpec((tm, d_att), row2))
        args.append(att)
    in_specs += [
        pl.BlockSpec((tm, d_cmlp), row2),
        pl.BlockSpec((tm, d_cmlp), row2),
        _const_spec(w["ws_rep"].shape),
        _const_spec(w["bs_rep"].shape),
        _const_spec(w["w_out"].shape),
        _const_spec((1, d)),
        _const_spec((1, d)),
    ]
    args += [u_act, v_ln, w["ws_rep"], w["bs_rep"], w["w_out"], w["ln1_g"], w["ln1_b"]]
    kern = functools.partial(_mix_kernel, absorbed=absorbed, alpha=alpha, run=run, d_att=d_att)
    return pl.pallas_call(
        kern, grid=(n_tok // tm,), in_specs=in_specs,
        out_specs=[pl.BlockSpec((bb, tt, d), row3), pl.BlockSpec((tm, d), row2)],
        out_shape=[jax.ShapeDtypeStruct((b, t, d), F32), jax.ShapeDtypeStruct((n_tok, d), BF16)],
        scratch_shapes=[pltpu.VMEM((tm, d_att + d_cmlp), BF16)],
        compiler_params=_params(("arbitrary",)),
        name="mix_sample" if absorbed else "mix_prompt",
    )(*args)


def _ffn_kernel(h_ref, x1_ref, g2_ref, wg_ref, wu_ref, wd_ref, lng_ref, lnb_ref, o_ref, acc_ref, *, alpha):
    f = pl.program_id(1)
    bb, tt, d = x1_ref.shape
    h = h_ref[...]
    a = (jax.nn.silu(_dot(h, wg_ref[...])) * _dot(h, wu_ref[...])).astype(BF16)
    part = _dot(a, wd_ref[...])

    @pl.when(f == 0)
    def _():
        acc_ref[...] = part

    @pl.when(f > 0)
    def _():
        acc_ref[...] += part

    @pl.when(f == pl.num_programs(1) - 1)
    def _():
        y = alpha * x1_ref[...] + g2_ref[...] * acc_ref[...].reshape(bb, tt, d)
        o_ref[...] = _layernorm(y, lng_ref[...], lnb_ref[...])


def _ffn(h2, x1, g2, w, *, absorbed, alpha, tm_target, tf_target):
    b, t, d = x1.shape
    d_ff = w["w_gate"].shape[1]
    if absorbed:
        tt, bb = t, _largest_tile(b, max(1, tm_target // t), 1)
    else:
        tt, bb = _largest_tile(t, tm_target, 16), 1
    nt = t // tt
    tm = bb * tt
    tf = _largest_tile(d_ff, tf_target, LANES)
    row3 = lambda i, f: (i // nt, i % nt, 0)
    return pl.pallas_call(
        functools.partial(_ffn_kernel, alpha=alpha),
        grid=(b * t // tm, d_ff // tf),
        in_specs=[
            pl.BlockSpec((tm, d), lambda i, f: (i, 0)),
            pl.BlockSpec((bb, tt, d), row3),
            pl.BlockSpec((bb, 1, d), lambda i, f: (i // nt, 0, 0)),
            pl.BlockSpec((d, tf), lambda i, f: (0, f)),
            pl.BlockSpec((d, tf), lambda i, f: (0, f)),
            pl.BlockSpec((tf, d), lambda i, f: (f, 0)),
            _const_spec((1, d)),
            _const_spec((1, d)),
        ],
        out_specs=pl.BlockSpec((bb, tt, d), row3),
        out_shape=jax.ShapeDtypeStruct((b, t, d), F32),
        scratch_shapes=[pltpu.VMEM((tm, d), F32)],
        compiler_params=_params(("arbitrary", "arbitrary")),
        name="ffn_sample" if absorbed else "ffn_prompt",
    )(h2, x1, g2, w["w_gate"], w["w_up"], w["w_down"], w["ln2_g"], w["ln2_b"])


def _rope_pair(a):
    half = D_ROPE // 2
    return jnp.concatenate([a, -a[..., half:], a[..., :half]], axis=-1)


def _layer_weights(l, w_in, q_norm, w_uq, kv_norm, w_uk, w_uv, v_ln_g, v_ln_b, w_s, b_s, w_out, ln1_g, ln1_b,
                   w_gate, w_up, w_down, ln2_g, ln2_b, t_prompt, t_sample, mb_sample):
    q_lora, kv_lora, d_cmlp = q_norm.shape[-1], kv_norm.shape[-1], v_ln_g.shape[-1]
    wi = w_in[l]
    i0, i1, i2, i3 = q_lora, q_lora + kv_lora, q_lora + kv_lora + D_ROPE, q_lora + kv_lora + D_ROPE + d_cmlp
    w_in_p = jnp.concatenate([wi[:, :i1], wi[:, i2:i3], wi[:, i3:], _rope_pair(wi[:, i1:i2])], axis=1).astype(BF16)
    wq = w_uq[l].reshape(q_lora, N_HEADS, D_NOPE + D_ROPE)
    w_q = jnp.concatenate([wq[..., :D_NOPE].reshape(q_lora, -1),
                           _rope_pair(wq[..., D_NOPE:]).reshape(q_lora, -1)], axis=1).astype(BF16)
    row = lambda a: a[l].reshape(1, -1)

    def mixing(t, mb):
        run = min(t, CHUNK)
        rep = mb // run
        return (jnp.tile(w_s[l][:, :run, :run], (1, rep, rep)), jnp.tile(b_s[l][:, :run], (1, rep)).T)

    ws_p, bs_p = mixing(t_prompt, min(t_prompt, CHUNK))
    ws_s, bs_s = mixing(t_sample, mb_sample)
    common = {
        "w_in": w_in_p, "w_q": w_q, "q_norm": row(q_norm), "kv_norm": row(kv_norm),
        "v_ln_g": row(v_ln_g), "v_ln_b": row(v_ln_b), "w_out": w_out[l].astype(BF16),
        "ln1_g": row(ln1_g), "ln1_b": row(ln1_b), "w_gate": w_gate[l].astype(BF16), "w_up": w_up[l].astype(BF16),
        "w_down": w_down[l].astype(BF16), "ln2_g": row(ln2_g), "ln2_b": row(ln2_b),
    }
    prompt = dict(common, w_uk_f=w_uk[l].reshape(kv_lora, -1).astype(BF16),
                  w_uv_f=w_uv[l].reshape(kv_lora, -1).astype(BF16), ws_rep=ws_p, bs_rep=bs_p)
    sample = dict(common, w_uk_t=jnp.transpose(w_uk[l], (1, 2, 0)).astype(BF16),
                  w_uv_h=jnp.transpose(w_uv[l], (1, 0, 2)).astype(BF16), ws_rep=ws_s, bs_rep=bs_s)
    return prompt, sample


def _rope_tables(pos):
    inv = ROPE_THETA ** (-jnp.arange(0, D_ROPE, 2, dtype=F32) / D_ROPE)
    ang = pos[:, None] * inv[None, :]
    z = jnp.zeros((pos.shape[0], D_ROPE), F32)
    cos, sin = jnp.cos(ang), jnp.sin(ang)
    return jnp.concatenate([cos, cos, z], axis=1), jnp.concatenate([sin, sin, z], axis=1)


def _mods(mod):
    return [m[:, None, :] for m in jnp.split(mod, 6, axis=-1)]


def kernel(x_prompt, x_sample, cache_latent, cache_krope, page_table, c_prompt, c_sample, w_ada, b_ada, w_in, q_norm,
           w_uq, kv_norm, w_uk, w_uv, v_ln_g, v_ln_b, w_s, b_s, w_out, ln1_g, ln1_b, w_gate, w_up, w_down, ln2_g,
           ln2_b):
    depth = w_in.shape[0]
    alpha = (2.0 * depth) ** 0.25
    bp, tp, d = x_prompt.shape
    bs, ts, _ = x_sample.shape
    past_len = page_table.shape[1] * PAGE_SIZE
    kv_lora = kv_norm.shape[-1]
    d_cmlp = v_ln_g.shape[-1]
    cdt = cache_latent.dtype

    xp = x_prompt.astype(F32)
    xs = x_sample.astype(F32)
    cos_p, sin_p = _rope_tables(jnp.arange(tp, dtype=F32))
    cos_s, sin_s = _rope_tables(past_len + jnp.arange(ts, dtype=F32))
    c_all = jnp.concatenate([c_prompt, c_sample], axis=0).astype(F32)
    tm_s = _largest_tile(bs, max(1, 256 // ts), 1) * ts

    lat_p, kr_p, v_p, lat_s, kr_s, v_s = [], [], [], [], [], []
    for l in range(depth):
        wp, ws = _layer_weights(l, w_in, q_norm, w_uq, kv_norm, w_uk, w_uv, v_ln_g, v_ln_b, w_s, b_s, w_out, ln1_g,
                                ln1_b, w_gate, w_up, w_down, ln2_g, ln2_b, tp, ts, tm_s)
        mod = _ada(c_all, w_ada[l].astype(BF16), b_ada[l])
        sh1, sc1, g1, sh2, sc2, g2 = _mods(mod[:bp])
        ckv, kr, u_act, v_ln, q_cat, k_cat, v_h = _inproj(xp, sc1, sh1, cos_p, sin_p, wp, absorbed=False,
                                                          tm_target=256)
        att = _flash(q_cat, k_cat, v_h, bp, tp)
        x1, h2 = _mix(xp, g1, sc2, sh2, att, u_act, v_ln, wp, absorbed=False, alpha=alpha, tm_target=256)
        xp = _ffn(h2, x1, g2, wp, absorbed=False, alpha=alpha, tm_target=512, tf_target=512)
        lat_p.append(ckv.reshape(bp, tp, kv_lora))
        kr_p.append(kr.reshape(bp, tp, D_ROPE))
        v_p.append(v_ln.reshape(bp, tp, d_cmlp)[:, tp - min(tp, CHUNK):])

        sh1, sc1, g1, sh2, sc2, g2 = _mods(mod[bp:])
        ckv, kr, u_act, v_ln, q_cat = _inproj(xs, sc1, sh1, cos_s, sin_s, ws, absorbed=True, tm_target=tm_s)
        o_lat = _paged(q_cat, ckv, kr, cache_latent, cache_krope, page_table, l)
        x1, h2 = _mix(xs, g1, sc2, sh2, o_lat, u_act, v_ln, ws, absorbed=True, alpha=alpha, tm_target=tm_s)
        xs = _ffn(h2, x1, g2, ws, absorbed=True, alpha=alpha, tm_target=512, tf_target=512)
        lat_s.append(ckv.reshape(bs, ts, kv_lora))
        kr_s.append(kr.reshape(bs, ts, D_ROPE))
        v_s.append(v_ln.reshape(bs, ts, d_cmlp))

    return (xp.astype(x_prompt.dtype), xs.astype(x_sample.dtype),
            jnp.stack(lat_p).astype(cdt), jnp.stack(kr_p).astype(cdt), jnp.stack(v_p).astype(cdt),
            jnp.stack(lat_s).astype(cdt), jnp.stack(kr_s).astype(cdt), jnp.stack(v_s).astype(cdt))
```

```python
import functools
import math

import jax
import jax.numpy as jnp
from jax import lax
from jax.experimental import pallas as pl
from jax.experimental.pallas import tpu as pltpu

N_HEADS = 8
D_NOPE = 128
D_ROPE = 64
D_V = 128
N_CMLP_HEADS = 8
CHUNK = 128
PAGE_SIZE = 128
ROPE_THETA = 10000.0
RMS_EPS = 1e-6
LN_EPS = 1e-5
NEG = -1e30
SM_SCALE = 1.0 / math.sqrt(D_NOPE + D_ROPE)

LANES = 128
SUBLANES = 8
D_QK = 2 * LANES
VMEM_LIMIT = 56 * 1024 * 1024

F32 = jnp.float32
BF16 = jnp.bfloat16


def _largest_tile(n, target, multiple):
    if n <= target:
        return n
    t = (target // multiple) * multiple
    while t >= multiple:
        if n % t == 0:
            return t
        t -= multiple
    raise ValueError(f"no tile for {n}")


def _const_spec(shape):
    nd = len(shape)
    return pl.BlockSpec(shape, lambda *_: (0,) * nd, pipeline_mode=pl.Buffered(1))


def _params(semantics):
    return pltpu.CompilerParams(dimension_semantics=semantics, vmem_limit_bytes=VMEM_LIMIT)


def _layernorm(x, g, b):
    mu = jnp.mean(x, axis=-1, keepdims=True)
    xc = x - mu
    var = jnp.mean(xc * xc, axis=-1, keepdims=True)
    return xc * lax.rsqrt(var + LN_EPS) * g + b


def _rmsnorm(x, g):
    return x * lax.rsqrt(jnp.mean(x * x, axis=-1, keepdims=True) + RMS_EPS) * g


def _dot(a, b):
    return jnp.dot(a, b, preferred_element_type=F32)


def _dot_nt(a, b):
    return lax.dot_general(a, b, (((1,), (1,)), ((), ())), preferred_element_type=F32)


def _ada_kernel(c_ref, w_ref, b_ref, o_ref):
    c = c_ref[...]
    o_ref[...] = _dot(jax.nn.silu(c).astype(BF16), w_ref[...].astype(BF16)) + b_ref[...]


def _ada(c_all, w_ada, b_ada):
    n, d = c_all.shape
    dout = w_ada.shape[1]
    tn = _largest_tile(dout, 1536, LANES)
    return pl.pallas_call(
        _ada_kernel,
        grid=(dout // tn,),
        in_specs=[
            _const_spec((n, d)),
            pl.BlockSpec((d, tn), lambda j: (0, j)),
            pl.BlockSpec((1, tn), lambda j: (0, j)),
        ],
        out_specs=pl.BlockSpec((n, tn), lambda j: (0, j)),
        out_shape=jax.ShapeDtypeStruct((n, dout), F32),
        compiler_params=_params(("arbitrary",)),
        name="ada",
    )(c_all, w_ada, b_ada.reshape(1, dout))


def _rope_tile(t, cos, sin, bb, tt):
    rolled = pltpu.roll(t, D_ROPE, axis=1)
    r = t.reshape(bb, tt, LANES) * cos[None] + rolled.reshape(bb, tt, LANES) * sin[None]
    return r.reshape(bb * tt, LANES)


def _inproj_kernel(*refs, absorbed, q_lora, kv_lora, d_cmlp):
    if absorbed:
        (x_ref, sc_ref, sh_ref, cos_ref, sin_ref, win_ref, qn_ref, kvn_ref, vg_ref, vb_ref, wq_ref, wk_ref,
         ckv_out, kr_out, u_out, v_out, q_out) = refs
    else:
        (x_ref, sc_ref, sh_ref, cos_ref, sin_ref, win_ref, qn_ref, kvn_ref, vg_ref, vb_ref, wq_ref, wk_ref, wv_ref,
         ckv_out, kr_out, u_out, v_out, q_out, k_out, vh_out) = refs
    bb, tt, d = x_ref.shape
    tm = bb * tt
    cos = cos_ref[...]
    sin = sin_ref[...]

    h = x_ref[...] * (1.0 + sc_ref[...]) + sh_ref[...]
    h = h.reshape(tm, d).astype(BF16)

    o_kv = q_lora
    o_u = o_kv + kv_lora
    o_v = o_u + d_cmlp
    o_kr = o_v + d_cmlp

    ckv = _rmsnorm(_dot(h, win_ref[:, o_kv:o_u]), kvn_ref[...])
    ckv_out[...] = ckv
    ckv_bf = ckv.astype(BF16)

    u_out[...] = jax.nn.gelu(_dot(h, win_ref[:, o_u:o_v]))
    v_out[...] = _layernorm(jax.nn.gelu(_dot(h, win_ref[:, o_v:o_kr])), vg_ref[...], vb_ref[...])

    kro = _rope_tile(_dot(h, win_ref[:, o_kr:o_kr + LANES]), cos, sin, bb, tt)
    kr_out[...] = kro[:, :D_ROPE]

    cq = _rmsnorm(_dot(h, win_ref[:, 0:q_lora]), qn_ref[...]).astype(BF16)
    n_nope = N_HEADS * D_NOPE
    q_nope = _dot(cq, wq_ref[:, 0:n_nope])
    q_rt = _dot(cq, wq_ref[:, n_nope:n_nope + N_HEADS * LANES])

    if absorbed:
        for hd in range(N_HEADS):
            qn_h = q_nope[:, hd * D_NOPE:(hd + 1) * D_NOPE].astype(BF16)
            q_lat = _dot(qn_h, wk_ref[hd]) * SM_SCALE
            r_h = _rope_tile(q_rt[:, hd * LANES:(hd + 1) * LANES], cos, sin, bb, tt) * SM_SCALE
            q_out[:, hd, :, 0:kv_lora] = q_lat.reshape(bb, tt, kv_lora)
            q_out[:, hd, :, kv_lora:kv_lora + LANES] = r_h.reshape(bb, tt, LANES)
    else:
        k_nope = _dot(ckv_bf, wk_ref[...])
        vh_out[...] = _dot(ckv_bf, wv_ref[...]).astype(BF16)
        kro_bf = kro.astype(BF16)
        for hd in range(N_HEADS):
            r_h = _rope_tile(q_rt[:, hd * LANES:(hd + 1) * LANES], cos, sin, bb, tt) * SM_SCALE
            q_out[:, hd * D_QK:hd * D_QK + D_NOPE] = (q_nope[:, hd * D_NOPE:(hd + 1) * D_NOPE] * SM_SCALE).astype(BF16)
            q_out[:, hd * D_QK + D_NOPE:(hd + 1) * D_QK] = r_h.astype(BF16)
            k_out[:, hd * D_QK:hd * D_QK + D_NOPE] = k_nope[:, hd * D_NOPE:(hd + 1) * D_NOPE].astype(BF16)
            k_out[:, hd * D_QK + D_NOPE:(hd + 1) * D_QK] = kro_bf


def _inproj(x, sc, sh, cos_t, sin_t, w, *, absorbed, tm_target):
    b, t, d = x.shape
    q_lora, kv_lora, d_cmlp = w["q_norm"].shape[-1], w["kv_norm"].shape[-1], w["v_ln_g"].shape[-1]
    if absorbed:
        tt, bb = t, _largest_tile(b, max(1, tm_target // t), 1)
    else:
        tt, bb = _largest_tile(t, tm_target, 16), 1
    nt = t // tt
    tm = bb * tt
    n_tok = b * t
    grid = (n_tok // tm,)

    row3 = lambda i: (i // nt, i % nt, 0)
    mod3 = lambda i: (i // nt, 0, 0)
    row2 = lambda i: (i, 0)
    in_specs = [
        pl.BlockSpec((bb, tt, d), row3),
        pl.BlockSpec((bb, 1, d), mod3),
        pl.BlockSpec((bb, 1, d), mod3),
        pl.BlockSpec((tt, LANES), lambda i: (i % nt, 0)),
        pl.BlockSpec((tt, LANES), lambda i: (i % nt, 0)),
        _const_spec(w["w_in"].shape),
        _const_spec((1, q_lora)),
        _const_spec((1, kv_lora)),
        _const_spec((1, d_cmlp)),
        _const_spec((1, d_cmlp)),
        _const_spec(w["w_q"].shape),
    ]
    args = [x, sc, sh, cos_t, sin_t, w["w_in"], w["q_norm"], w["kv_norm"], w["v_ln_g"], w["v_ln_b"], w["w_q"]]
    out_specs = [
        pl.BlockSpec((tm, kv_lora), row2),
        pl.BlockSpec((tm, D_ROPE), row2),
        pl.BlockSpec((tm, d_cmlp), row2),
        pl.BlockSpec((tm, d_cmlp), row2),
    ]
    out_shape = [
        jax.ShapeDtypeStruct((n_tok, kv_lora), F32),
        jax.ShapeDtypeStruct((n_tok, D_ROPE), F32),
        jax.ShapeDtypeStruct((n_tok, d_cmlp), F32),
        jax.ShapeDtypeStruct((n_tok, d_cmlp), F32),
    ]
    if absorbed:
        in_specs.append(_const_spec(w["w_uk_t"].shape))
        args.append(w["w_uk_t"])
        d_q = kv_lora + LANES
        out_specs.append(pl.BlockSpec((bb, N_HEADS, t, d_q), lambda i: (i, 0, 0, 0)))
        out_shape.append(jax.ShapeDtypeStruct((b, N_HEADS, t, d_q), F32))
    else:
        in_specs += [_const_spec(w["w_uk_f"].shape), _const_spec(w["w_uv_f"].shape)]
        args += [w["w_uk_f"], w["w_uv_f"]]
        out_specs += [
            pl.BlockSpec((tm, N_HEADS * D_QK), row2),
            pl.BlockSpec((tm, N_HEADS * D_QK), row2),
            pl.BlockSpec((tm, N_HEADS * D_V), row2),
        ]
        out_shape += [
            jax.ShapeDtypeStruct((n_tok, N_HEADS * D_QK), BF16),
            jax.ShapeDtypeStruct((n_tok, N_HEADS * D_QK), BF16),
            jax.ShapeDtypeStruct((n_tok, N_HEADS * D_V), BF16),
        ]
    kern = functools.partial(_inproj_kernel, absorbed=absorbed, q_lora=q_lora, kv_lora=kv_lora, d_cmlp=d_cmlp)
    return pl.pallas_call(
        kern, grid=grid, in_specs=in_specs, out_specs=out_specs, out_shape=out_shape,
        compiler_params=_params(("arbitrary",)),
        name="inproj_sample" if absorbed else "inproj_prompt",
    )(*args)


def _flash_kernel(q_ref, k_ref, v_ref, o_ref, *, bucket):
    tq = q_ref.shape[0]
    t = k_ref.shape[0]
    qi = pl.program_id(2)
    q = q_ref[...]
    last_bucket = ((qi + 1) * tq - 1) // bucket
    for c in range(t // bucket):
        nk = (c + 1) * bucket

        @pl.when(last_bucket == c)
        def _(nk=nk):
            s = _dot_nt(q, k_ref[0:nk, :])
            row = lax.broadcasted_iota(jnp.int32, (tq, nk), 0) + qi * tq
            col = lax.broadcasted_iota(jnp.int32, (tq, nk), 1)
            s = jnp.where(col <= row, s, NEG)
            p = jnp.exp(s - jnp.max(s, axis=-1, keepdims=True))
            l = jnp.sum(p, axis=-1, keepdims=True)
            o_ref[...] = (_dot(p.astype(BF16), v_ref[0:nk, :]) / l).astype(o_ref.dtype)


def _flash(q_cat, k_cat, v_h, b, t):
    tq = _largest_tile(t, 256, 16)
    nq = t // tq
    bucket = _largest_tile(t, 512, tq)
    return pl.pallas_call(
        functools.partial(_flash_kernel, bucket=bucket),
        grid=(b, N_HEADS, nq),
        in_specs=[
            pl.BlockSpec((tq, D_QK), lambda bi, h, i: (bi * nq + i, h)),
            pl.BlockSpec((t, D_QK), lambda bi, h, i: (bi, h)),
            pl.BlockSpec((t, D_V), lambda bi, h, i: (bi, h)),
        ],
        out_specs=pl.BlockSpec((tq, D_V), lambda bi, h, i: (bi * nq + i, h)),
        out_shape=jax.ShapeDtypeStruct((b * t, N_HEADS * D_V), BF16),
        compiler_params=_params(("arbitrary", "arbitrary", "arbitrary")),
        name="flash_prompt",
    )(q_cat, k_cat, v_h)


def _paged_kernel(pt_ref, q_ref, ckv_ref, kr_ref, lat_hbm, krp_hbm, o_ref, lat_buf, krp_buf, sem, m_ref, l_ref, acc_ref,
                  *, layer, n_group, n_steps, kv_lora):
    n = pl.program_id(0)
    total = pl.num_programs(0)
    j = lax.rem(n, n_steps)
    slot = lax.rem(n, 2)
    s_new = ckv_ref.shape[1]
    n_rows = N_HEADS * s_new

    def page_copies(step, buf):
        bi = lax.div(step, n_steps)
        first = lax.rem(step, n_steps) * n_group
        copies = []
        for g in range(n_group):
            page = pt_ref[bi, first + g]
            rows = pl.ds(g * PAGE_SIZE, PAGE_SIZE)
            copies.append(pltpu.make_async_copy(lat_hbm.at[layer, page], lat_buf.at[buf, rows, :], sem.at[0, buf]))
            copies.append(pltpu.make_async_copy(krp_hbm.at[layer, page], krp_buf.at[buf, :, rows], sem.at[1, buf]))
        return copies

    @pl.when(n == 0)
    def _():
        for cp in page_copies(n, slot):
            cp.start()

    @pl.when(n + 1 < total)
    def _():
        for cp in page_copies(n + 1, 1 - slot):
            cp.start()

    q = q_ref[0].reshape(n_rows, kv_lora + LANES)
    q_lat = q[:, :kv_lora].astype(BF16)
    q_rope = q[:, kv_lora:kv_lora + D_ROPE].astype(BF16)

    def update(s, kv_bf):
        m_old = m_ref[...]
        m_new = jnp.maximum(m_old, jnp.max(s, axis=-1, keepdims=True))
        corr = jnp.exp(m_old - m_new)
        p = jnp.exp(s - m_new)
        l_ref[...] = l_ref[...] * corr + jnp.sum(p, axis=-1, keepdims=True)
        acc_ref[...] = acc_ref[...] * corr + _dot(p.astype(BF16), kv_bf)
        m_ref[...] = m_new

    @pl.when(j == 0)
    def _():
        pad = PAGE_SIZE - s_new
        kv = jnp.concatenate([ckv_ref[0], jnp.zeros((pad, kv_lora), F32)], axis=0).astype(BF16)
        kr = jnp.concatenate([kr_ref[0], jnp.zeros((pad, D_ROPE), F32)], axis=0).astype(BF16)
        s = _dot_nt(q_lat, kv) + _dot_nt(q_rope, kr)
        qpos = lax.broadcasted_iota(jnp.int32, (n_rows, PAGE_SIZE), 0) % s_new
        kpos = lax.broadcasted_iota(jnp.int32, (n_rows, PAGE_SIZE), 1)
        s = jnp.where(kpos <= qpos, s, NEG)
        m_ref[...] = jnp.full(m_ref.shape, NEG, F32)
        l_ref[...] = jnp.zeros(l_ref.shape, F32)
        acc_ref[...] = jnp.zeros(acc_ref.shape, F32)
        update(s, kv)

    for cp in page_copies(n, slot):
        cp.wait()
    kv = lat_buf[slot].astype(BF16)
    kr_t = krp_buf[slot].astype(BF16)
    update(_dot_nt(q_lat, kv) + _dot(q_rope, kr_t), kv)

    @pl.when(j == n_steps - 1)
    def _():
        o = acc_ref[...] / l_ref[...]
        o_ref[0] = o.reshape(N_HEADS, s_new, kv_lora)


def _paged(q_cat, ckv_new, kr_new, cache_latent, cache_krope_t, page_table, layer):
    b, _, s_new, d_q = q_cat.shape
    kv_lora = ckv_new.shape[-1]
    n_pages = page_table.shape[1]
    n_group = _largest_tile(n_pages, 16, 1)
    n_steps = n_pages // n_group
    n_rows = N_HEADS * s_new
    grid_spec = pltpu.PrefetchScalarGridSpec(
        num_scalar_prefetch=1,
        grid=(b * n_steps,),
        in_specs=[
            pl.BlockSpec((1, N_HEADS, s_new, d_q), lambda n, pt: (n // n_steps, 0, 0, 0)),
            pl.BlockSpec((1, s_new, kv_lora), lambda n, pt: (n // n_steps, 0, 0)),
            pl.BlockSpec((1, s_new, D_ROPE), lambda n, pt: (n // n_steps, 0, 0)),
            pl.BlockSpec(memory_space=pl.ANY),
            pl.BlockSpec(memory_space=pl.ANY),
        ],
        out_specs=pl.BlockSpec((1, N_HEADS, s_new, kv_lora), lambda n, pt: (n // n_steps, 0, 0, 0)),
        scratch_shapes=[
            pltpu.VMEM((2, n_group * PAGE_SIZE, kv_lora), F32),
            pltpu.VMEM((2, D_ROPE, n_group * PAGE_SIZE), F32),
            pltpu.SemaphoreType.DMA((2, 2)),
            pltpu.VMEM((n_rows, 1), F32),
            pltpu.VMEM((n_rows, 1), F32),
            pltpu.VMEM((n_rows, kv_lora), F32),
        ],
    )
    kern = functools.partial(_paged_kernel, layer=layer, n_group=n_group, n_steps=n_steps, kv_lora=kv_lora)
    return pl.pallas_call(
        kern, grid_spec=grid_spec,
        out_shape=jax.ShapeDtypeStruct((b, N_HEADS, s_new, kv_lora), F32),
        compiler_params=_params(("arbitrary",)),
        name="paged_sample",
    )(page_table, q_cat, ckv_new.reshape(b, s_new, kv_lora), kr_new.reshape(b, s_new, D_ROPE),
      cache_latent, cache_krope_t)


def _mix_kernel(*refs, absorbed, alpha, run, d_att):
    if absorbed:
        (x_ref, g1_ref, sc2_ref, sh2_ref, att_ref, wuv_ref, u_ref, v_ref, ws_ref, bs_ref, wout_ref, lng_ref, lnb_ref,
         x1_out, h2_out, cat_ref) = refs
    else:
        (x_ref, g1_ref, sc2_ref, sh2_ref, att_ref, u_ref, v_ref, ws_ref, bs_ref, wout_ref, lng_ref, lnb_ref,
         x1_out, h2_out, cat_ref) = refs
    bb, tt, d = x_ref.shape
    tm = bb * tt
    mb = ws_ref.shape[1]
    d_cmlp = u_ref.shape[1]
    d_head = d_cmlp // N_CMLP_HEADS

    if absorbed:
        for hd in range(N_HEADS):
            o_lat = att_ref[:, hd].reshape(tm, att_ref.shape[-1]).astype(BF16)
            cat_ref[:, hd * D_V:(hd + 1) * D_V] = _dot(o_lat, wuv_ref[hd]).astype(BF16)
    else:
        cat_ref[:, 0:d_att] = att_ref[...]

    r = lax.broadcasted_iota(jnp.int32, (mb, mb), 0)
    c = lax.broadcasted_iota(jnp.int32, (mb, mb), 1)
    shift = run.bit_length() - 1
    keep = ((r >> shift) == (c >> shift)) & ((c & (run - 1)) <= (r & (run - 1)))
    bs = bs_ref[...]
    for hd in range(N_CMLP_HEADS):
        wm = jnp.where(keep, ws_ref[hd], 0.0).astype(BF16)
        bias = bs[:, hd:hd + 1]
        cols = slice(hd * d_head, (hd + 1) * d_head)
        for blk in range(tm // mb):
            rows = slice(blk * mb, (blk + 1) * mb)
            mixed = _dot(wm, v_ref[rows, cols].astype(BF16)) + bias
            cat_ref[rows, d_att + hd * d_head:d_att + (hd + 1) * d_head] = (u_ref[rows, cols] * mixed).astype(BF16)

    mix = _dot(cat_ref[...], wout_ref[...])
    y = alpha * x_ref[...] + g1_ref[...] * mix.reshape(bb, tt, d)
    x1 = _layernorm(y, lng_ref[...], lnb_ref[...])
    x1_out[...] = x1
    h2 = x1 * (1.0 + sc2_ref[...]) + sh2_ref[...]
    h2_out[...] = h2.reshape(tm, d).astype(BF16)


def _mix(x, g1, sc2, sh2, att, u_act, v_ln, w, *, absorbed, alpha, tm_target):
    b, t, d = x.shape
    d_cmlp = u_act.shape[1]
    d_att = N_HEADS * D_V
    run = min(t, CHUNK)
    assert run & (run - 1) == 0 and t % run == 0
    if absorbed:
        tt, bb = t, _largest_tile(b, max(1, tm_target // t), 1)
    else:
        tt, bb = _largest_tile(t, tm_target, CHUNK), 1
    nt = t // tt
    tm = bb * tt
    mb = w["ws_rep"].shape[1]
    assert tm % mb == 0
    n_tok = b * t
    row3 = lambda i: (i // nt, i % nt, 0)
    mod3 = lambda i: (i // nt, 0, 0)
    row2 = lambda i: (i, 0)
    in_specs = [
        pl.BlockSpec((bb, tt, d), row3),
        pl.BlockSpec((bb, 1, d), mod3),
        pl.BlockSpec((bb, 1, d), mod3),
        pl.BlockSpec((bb, 1, d), mod3),
    ]
    args = [x, g1, sc2, sh2]
    if absorbed:
        in_specs += [pl.BlockSpec((bb, N_HEADS, t, att.shape[-1]), lambda i: (i, 0, 0, 0)),
                     _const_spec(w["w_uv_h"].shape)]
        args += [att, w["w_uv_h"]]
    else:
        in_specs.append(pl.BlockSpec((tm, d_att), row2))
        args.append(att)
    in_specs += [
        pl.BlockSpec((tm, d_cmlp), row2),
        pl.BlockSpec((tm, d_cmlp), row2),
        _const_spec(w["ws_rep"].shape),
        _const_spec(w["bs_rep"].shape),
        _const_spec(w["w_out"].shape),
        _const_spec((1, d)),
        _const_spec((1, d)),
    ]
    args += [u_act, v_ln, w["ws_rep"], w["bs_rep"], w["w_out"], w["ln1_g"], w["ln1_b"]]
    kern = functools.partial(_mix_kernel, absorbed=absorbed, alpha=alpha, run=run, d_att=d_att)
    return pl.pallas_call(
        kern, grid=(n_tok // tm,), in_specs=in_specs,
        out_specs=[pl.BlockSpec((bb, tt, d), row3), pl.BlockSpec((tm, d), row2)],
        out_shape=[jax.ShapeDtypeStruct((b, t, d), F32), jax.ShapeDtypeStruct((n_tok, d), BF16)],
        scratch_shapes=[pltpu.VMEM((tm, d_att + d_cmlp), BF16)],
        compiler_params=_params(("arbitrary",)),
        name="mix_sample" if absorbed else "mix_prompt",
    )(*args)


def _ffn_kernel(h_ref, x1_ref, g2_ref, wg_ref, wu_ref, wd_ref, lng_ref, lnb_ref, o_ref, acc_ref, *, alpha):
    f = pl.program_id(1)
    bb, tt, d = x1_ref.shape
    @pl.when(f == 0)
    def _():
        acc_ref[...] = jnp.zeros(acc_ref.shape, F32)

    h = h_ref[...]
    a = (jax.nn.silu(_dot(h, wg_ref[...])) * _dot(h, wu_ref[...])).astype(BF16)
    acc_ref[...] += _dot(a, wd_ref[...])

    @pl.when(f == pl.num_programs(1) - 1)
    def _():
        y = alpha * x1_ref[...] + g2_ref[...] * acc_ref[...].reshape(bb, tt, d)
        o_ref[...] = _layernorm(y, lng_ref[...], lnb_ref[...])


def _ffn(h2, x1, g2, w, *, absorbed, alpha, tm_target, tf_target):
    b, t, d = x1.shape
    d_ff = w["w_gate"].shape[1]
    if absorbed:
        tt, bb = t, _largest_tile(b, max(1, tm_target // t), 1)
    else:
        tt, bb = _largest_tile(t, tm_target, 16), 1
    nt = t // tt
    tm = bb * tt
    tf = _largest_tile(d_ff, tf_target, LANES)
    row3 = lambda i, f: (i // nt, i % nt, 0)
    return pl.pallas_call(
        functools.partial(_ffn_kernel, alpha=alpha),
        grid=(b * t // tm, d_ff // tf),
        in_specs=[
            pl.BlockSpec((tm, d), lambda i, f: (i, 0)),
            pl.BlockSpec((bb, tt, d), row3),
            pl.BlockSpec((bb, 1, d), lambda i, f: (i // nt, 0, 0)),
            pl.BlockSpec((d, tf), lambda i, f: (0, f)),
            pl.BlockSpec((d, tf), lambda i, f: (0, f)),
            pl.BlockSpec((tf, d), lambda i, f: (f, 0)),
            _const_spec((1, d)),
            _const_spec((1, d)),
        ],
        out_specs=pl.BlockSpec((bb, tt, d), row3),
        out_shape=jax.ShapeDtypeStruct((b, t, d), F32),
        scratch_shapes=[pltpu.VMEM((tm, d), F32)],
        compiler_params=_params(("arbitrary", "arbitrary")),
        name="ffn_sample" if absorbed else "ffn_prompt",
    )(h2, x1, g2, w["w_gate"], w["w_up"], w["w_down"], w["ln2_g"], w["ln2_b"])


def _rope_pair(a):
    half = D_ROPE // 2
    return jnp.concatenate([a, -a[..., half:], a[..., :half]], axis=-1)


def _layer_weights(l, w_in, q_norm, w_uq, kv_norm, w_uk, w_uv, v_ln_g, v_ln_b, w_s, b_s, w_out, ln1_g, ln1_b,
                   w_gate, w_up, w_down, ln2_g, ln2_b, t_prompt, t_sample, mb_sample):
    q_lora, kv_lora, d_cmlp = q_norm.shape[-1], kv_norm.shape[-1], v_ln_g.shape[-1]
    wi = w_in[l]
    i0, i1, i2, i3 = q_lora, q_lora + kv_lora, q_lora + kv_lora + D_ROPE, q_lora + kv_lora + D_ROPE + d_cmlp
    w_in_p = jnp.concatenate([wi[:, :i1], wi[:, i2:i3], wi[:, i3:], _rope_pair(wi[:, i1:i2])], axis=1).astype(BF16)
    wq = w_uq[l].reshape(q_lora, N_HEADS, D_NOPE + D_ROPE)
    w_q = jnp.concatenate([wq[..., :D_NOPE].reshape(q_lora, -1),
                           _rope_pair(wq[..., D_NOPE:]).reshape(q_lora, -1)], axis=1).astype(BF16)
    row = lambda a: a[l].reshape(1, -1)

    def mixing(t, mb):
        run = min(t, CHUNK)
        rep = mb // run
        return (jnp.tile(w_s[l][:, :run, :run], (1, rep, rep)), jnp.tile(b_s[l][:, :run], (1, rep)).T)

    ws_p, bs_p = mixing(t_prompt, min(t_prompt, CHUNK))
    ws_s, bs_s = mixing(t_sample, mb_sample)
    common = {
        "w_in": w_in_p, "w_q": w_q, "q_norm": row(q_norm), "kv_norm": row(kv_norm),
        "v_ln_g": row(v_ln_g), "v_ln_b": row(v_ln_b), "w_out": w_out[l].astype(BF16),
        "ln1_g": row(ln1_g), "ln1_b": row(ln1_b), "w_gate": w_gate[l].astype(BF16), "w_up": w_up[l].astype(BF16),
        "w_down": w_down[l].astype(BF16), "ln2_g": row(ln2_g), "ln2_b": row(ln2_b),
    }
    prompt = dict(common, w_uk_f=w_uk[l].reshape(kv_lora, -1).astype(BF16),
                  w_uv_f=w_uv[l].reshape(kv_lora, -1).astype(BF16), ws_rep=ws_p, bs_rep=bs_p)
    sample = dict(common, w_uk_t=jnp.transpose(w_uk[l], (1, 2, 0)).astype(BF16),
                  w_uv_h=jnp.transpose(w_uv[l], (1, 0, 2)).astype(BF16), ws_rep=ws_s, bs_rep=bs_s)
    return prompt, sample


def _rope_tables(pos):
    inv = ROPE_THETA ** (-jnp.arange(0, D_ROPE, 2, dtype=F32) / D_ROPE)
    ang = pos[:, None] * inv[None, :]
    z = jnp.zeros((pos.shape[0], D_ROPE), F32)
    cos, sin = jnp.cos(ang), jnp.sin(ang)
    return jnp.concatenate([cos, cos, z], axis=1), jnp.concatenate([sin, sin, z], axis=1)


def _mods(mod):
    return [m[:, None, :] for m in jnp.split(mod, 6, axis=-1)]


def kernel(x_prompt, x_sample, cache_latent, cache_krope, page_table, c_prompt, c_sample, w_ada, b_ada, w_in, q_norm,
           w_uq, kv_norm, w_uk, w_uv, v_ln_g, v_ln_b, w_s, b_s, w_out, ln1_g, ln1_b, w_gate, w_up, w_down, ln2_g,
           ln2_b):
    depth = w_in.shape[0]
    alpha = (2.0 * depth) ** 0.25
    bp, tp, d = x_prompt.shape
    bs, ts, _ = x_sample.shape
    past_len = page_table.shape[1] * PAGE_SIZE
    kv_lora = kv_norm.shape[-1]
    d_cmlp = v_ln_g.shape[-1]
    cdt = cache_latent.dtype

    xp = x_prompt.astype(F32)
    xs = x_sample.astype(F32)
    cos_p, sin_p = _rope_tables(jnp.arange(tp, dtype=F32))
    cos_s, sin_s = _rope_tables(past_len + jnp.arange(ts, dtype=F32))
    c_all = jnp.concatenate([c_prompt, c_sample], axis=0).astype(F32)
    krope_t = jnp.swapaxes(cache_krope, 2, 3)
    tm_s = _largest_tile(bs, max(1, 256 // ts), 1) * ts

    lat_p, kr_p, v_p, lat_s, kr_s, v_s = [], [], [], [], [], []
    for l in range(depth):
        wp, ws = _layer_weights(l, w_in, q_norm, w_uq, kv_norm, w_uk, w_uv, v_ln_g, v_ln_b, w_s, b_s, w_out, ln1_g,
                                ln1_b, w_gate, w_up, w_down, ln2_g, ln2_b, tp, ts, tm_s)
        mod = _ada(c_all, w_ada[l], b_ada[l])
        sh1, sc1, g1, sh2, sc2, g2 = _mods(mod[:bp])
        ckv, kr, u_act, v_ln, q_cat, k_cat, v_h = _inproj(xp, sc1, sh1, cos_p, sin_p, wp, absorbed=False,
                                                          tm_target=256)
        att = _flash(q_cat, k_cat, v_h, bp, tp)
        x1, h2 = _mix(xp, g1, sc2, sh2, att, u_act, v_ln, wp, absorbed=False, alpha=alpha, tm_target=256)
        xp = _ffn(h2, x1, g2, wp, absorbed=False, alpha=alpha, tm_target=512, tf_target=512)
        lat_p.append(ckv.reshape(bp, tp, kv_lora))
        kr_p.append(kr.reshape(bp, tp, D_ROPE))
        v_p.append(v_ln.reshape(bp, tp, d_cmlp)[:, tp - min(tp, CHUNK):])

        sh1, sc1, g1, sh2, sc2, g2 = _mods(mod[bp:])
        ckv, kr, u_act, v_ln, q_cat = _inproj(xs, sc1, sh1, cos_s, sin_s, ws, absorbed=True, tm_target=tm_s)
        o_lat = _paged(q_cat, ckv, kr, cache_latent, krope_t, page_table, l)
        x1, h2 = _mix(xs, g1, sc2, sh2, o_lat, u_act, v_ln, ws, absorbed=True, alpha=alpha, tm_target=tm_s)
        xs = _ffn(h2, x1, g2, ws, absorbed=True, alpha=alpha, tm_target=512, tf_target=512)
        lat_s.append(ckv.reshape(bs, ts, kv_lora))
        kr_s.append(kr.reshape(bs, ts, D_ROPE))
        v_s.append(v_ln.reshape(bs, ts, d_cmlp))

    return (xp.astype(x_prompt.dtype), xs.astype(x_sample.dtype),
            jnp.stack(lat_p).astype(cdt), jnp.stack(kr_p).astype(cdt), jnp.stack(v_p).astype(cdt),
            jnp.stack(lat_s).astype(cdt), jnp.stack(kr_s).astype(cdt), jnp.stack(v_s).astype(cdt))
```

```python
import functools
import math

import jax
import jax.numpy as jnp
from jax import lax
from jax.experimental import pallas as pl
from jax.experimental.pallas import tpu as pltpu

N_HEADS = 8
D_NOPE = 128
D_ROPE = 64
D_V = 128
N_CMLP_HEADS = 8
CHUNK = 128
PAGE_SIZE = 128
ROPE_THETA = 10000.0
RMS_EPS = 1e-6
LN_EPS = 1e-5
NEG = -1e30
SM_SCALE = 1.0 / math.sqrt(D_NOPE + D_ROPE)

LANES = 128
SUBLANES = 8
D_QK = 2 * LANES
VMEM_LIMIT = 56 * 1024 * 1024

F32 = jnp.float32
BF16 = jnp.bfloat16


def _largest_tile(n, target, multiple):
    if n <= target:
        return n
    t = (target // multiple) * multiple
    while t >= multiple:
        if n % t == 0:
            return t
        t -= multiple
    raise ValueError(f"no tile for {n}")


def _const_spec(shape):
    nd = len(shape)
    return pl.BlockSpec(shape, lambda *_: (0,) * nd, pipeline_mode=pl.Buffered(1))


def _params(semantics):
    return pltpu.CompilerParams(dimension_semantics=semantics, vmem_limit_bytes=VMEM_LIMIT)


def _layernorm(x, g, b):
    mu = jnp.mean(x, axis=-1, keepdims=True)
    xc = x - mu
    var = jnp.mean(xc * xc, axis=-1, keepdims=True)
    return xc * lax.rsqrt(var + LN_EPS) * g + b


def _rmsnorm(x, g):
    return x * lax.rsqrt(jnp.mean(x * x, axis=-1, keepdims=True) + RMS_EPS) * g


def _dot(a, b):
    return jnp.dot(a, b, preferred_element_type=F32)


def _dot_nt(a, b):
    return lax.dot_general(a, b, (((1,), (1,)), ((), ())), preferred_element_type=F32)


def _ada_kernel(c_ref, w_ref, b_ref, o_ref):
    c = c_ref[...]
    o_ref[...] = _dot(jax.nn.silu(c).astype(BF16), w_ref[...].astype(BF16)) + b_ref[...]


def _ada(c_all, w_ada, b_ada):
    n, d = c_all.shape
    dout = w_ada.shape[1]
    tn = _largest_tile(dout, 1536, LANES)
    return pl.pallas_call(
        _ada_kernel,
        grid=(dout // tn,),
        in_specs=[
            _const_spec((n, d)),
            pl.BlockSpec((d, tn), lambda j: (0, j)),
            pl.BlockSpec((1, tn), lambda j: (0, j)),
        ],
        out_specs=pl.BlockSpec((n, tn), lambda j: (0, j)),
        out_shape=jax.ShapeDtypeStruct((n, dout), F32),
        compiler_params=_params(("arbitrary",)),
        name="ada",
    )(c_all, w_ada, b_ada.reshape(1, dout))


def _rope_tile(t, cos, sin, bb, tt):
    rolled = pltpu.roll(t, D_ROPE, axis=1)
    r = t.reshape(bb, tt, LANES) * cos[None] + rolled.reshape(bb, tt, LANES) * sin[None]
    return r.reshape(bb * tt, LANES)


def _inproj_kernel(*refs, absorbed, q_lora, kv_lora, d_cmlp):
    if absorbed:
        (x_ref, sc_ref, sh_ref, cos_ref, sin_ref, wlat_ref, wuv_ref, wkr_ref, qn_ref, kvn_ref, vg_ref, vb_ref,
         wq_ref, wk_ref, ckv_out, kr_out, u_out, v_out, q_out) = refs
    else:
        (x_ref, sc_ref, sh_ref, cos_ref, sin_ref, wlat_ref, wuv_ref, wkr_ref, qn_ref, kvn_ref, vg_ref, vb_ref,
         wq_ref, wk_ref, wv_ref, ckv_out, kr_out, u_out, v_out, q_out, k_out, vh_out) = refs
    bb, tt, d = x_ref.shape
    tm = bb * tt
    cos = cos_ref[...]
    sin = sin_ref[...]

    h = x_ref[...] * (1.0 + sc_ref[...]) + sh_ref[...]
    h = h.reshape(tm, d).astype(BF16)

    ckv = _rmsnorm(_dot(h, wlat_ref[:, q_lora:q_lora + kv_lora]), kvn_ref[...])
    ckv_out[...] = ckv
    ckv_bf = ckv.astype(BF16)

    u_out[...] = jax.nn.gelu(_dot(h, wuv_ref[:, 0:d_cmlp]))
    v_out[...] = _layernorm(jax.nn.gelu(_dot(h, wuv_ref[:, d_cmlp:2 * d_cmlp])), vg_ref[...], vb_ref[...])

    kro = _rope_tile(_dot(h, wkr_ref[...]), cos, sin, bb, tt)
    kr_out[...] = kro[:, :D_ROPE]

    cq = _rmsnorm(_dot(h, wlat_ref[:, 0:q_lora]), qn_ref[...]).astype(BF16)
    n_nope = N_HEADS * D_NOPE
    q_nope = _dot(cq, wq_ref[:, 0:n_nope])
    q_rt = _dot(cq, wq_ref[:, n_nope:n_nope + N_HEADS * LANES])

    if absorbed:
        for hd in range(N_HEADS):
            qn_h = q_nope[:, hd * D_NOPE:(hd + 1) * D_NOPE].astype(BF16)
            q_lat = _dot(qn_h, wk_ref[hd]) * SM_SCALE
            r_h = _rope_tile(q_rt[:, hd * LANES:(hd + 1) * LANES], cos, sin, bb, tt) * SM_SCALE
            q_out[:, hd, :, 0:kv_lora] = q_lat.reshape(bb, tt, kv_lora)
            q_out[:, hd, :, kv_lora:kv_lora + LANES] = r_h.reshape(bb, tt, LANES)
    else:
        k_nope = _dot(ckv_bf, wk_ref[...])
        vh_out[...] = _dot(ckv_bf, wv_ref[...]).astype(BF16)
        kro_bf = kro.astype(BF16)
        for hd in range(N_HEADS):
            r_h = _rope_tile(q_rt[:, hd * LANES:(hd + 1) * LANES], cos, sin, bb, tt) * SM_SCALE
            q_out[:, hd * D_QK:hd * D_QK + D_NOPE] = (q_nope[:, hd * D_NOPE:(hd + 1) * D_NOPE] * SM_SCALE).astype(BF16)
            q_out[:, hd * D_QK + D_NOPE:(hd + 1) * D_QK] = r_h.astype(BF16)
            k_out[:, hd * D_QK:hd * D_QK + D_NOPE] = k_nope[:, hd * D_NOPE:(hd + 1) * D_NOPE].astype(BF16)
            k_out[:, hd * D_QK + D_NOPE:(hd + 1) * D_QK] = kro_bf


def _inproj(x, sc, sh, cos_t, sin_t, w, *, absorbed, tm_target):
    b, t, d = x.shape
    q_lora, kv_lora, d_cmlp = w["q_norm"].shape[-1], w["kv_norm"].shape[-1], w["v_ln_g"].shape[-1]
    if absorbed:
        tt, bb = t, _largest_tile(b, max(1, tm_target // t), 1)
    else:
        tt, bb = _largest_tile(t, tm_target, 16), 1
    nt = t // tt
    tm = bb * tt
    n_tok = b * t
    grid = (n_tok // tm,)

    row3 = lambda i: (i // nt, i % nt, 0)
    mod3 = lambda i: (i // nt, 0, 0)
    row2 = lambda i: (i, 0)
    in_specs = [
        pl.BlockSpec((bb, tt, d), row3),
        pl.BlockSpec((bb, 1, d), mod3),
        pl.BlockSpec((bb, 1, d), mod3),
        pl.BlockSpec((tt, LANES), lambda i: (i % nt, 0)),
        pl.BlockSpec((tt, LANES), lambda i: (i % nt, 0)),
        _const_spec(w["w_lat"].shape),
        _const_spec(w["w_uv_in"].shape),
        _const_spec(w["w_kr"].shape),
        _const_spec((1, q_lora)),
        _const_spec((1, kv_lora)),
        _const_spec((1, d_cmlp)),
        _const_spec((1, d_cmlp)),
        _const_spec(w["w_q"].shape),
    ]
    args = [x, sc, sh, cos_t, sin_t, w["w_lat"], w["w_uv_in"], w["w_kr"], w["q_norm"], w["kv_norm"], w["v_ln_g"],
            w["v_ln_b"], w["w_q"]]
    out_specs = [
        pl.BlockSpec((tm, kv_lora), row2),
        pl.BlockSpec((tm, D_ROPE), row2),
        pl.BlockSpec((tm, d_cmlp), row2),
        pl.BlockSpec((tm, d_cmlp), row2),
    ]
    out_shape = [
        jax.ShapeDtypeStruct((n_tok, kv_lora), F32),
        jax.ShapeDtypeStruct((n_tok, D_ROPE), F32),
        jax.ShapeDtypeStruct((n_tok, d_cmlp), F32),
        jax.ShapeDtypeStruct((n_tok, d_cmlp), F32),
    ]
    if absorbed:
        in_specs.append(_const_spec(w["w_uk_t"].shape))
        args.append(w["w_uk_t"])
        d_q = kv_lora + LANES
        out_specs.append(pl.BlockSpec((bb, N_HEADS, t, d_q), lambda i: (i, 0, 0, 0)))
        out_shape.append(jax.ShapeDtypeStruct((b, N_HEADS, t, d_q), F32))
    else:
        in_specs += [_const_spec(w["w_uk_f"].shape), _const_spec(w["w_uv_f"].shape)]
        args += [w["w_uk_f"], w["w_uv_f"]]
        out_specs += [
            pl.BlockSpec((tm, N_HEADS * D_QK), row2),
            pl.BlockSpec((tm, N_HEADS * D_QK), row2),
            pl.BlockSpec((tm, N_HEADS * D_V), row2),
        ]
        out_shape += [
            jax.ShapeDtypeStruct((n_tok, N_HEADS * D_QK), BF16),
            jax.ShapeDtypeStruct((n_tok, N_HEADS * D_QK), BF16),
            jax.ShapeDtypeStruct((n_tok, N_HEADS * D_V), BF16),
        ]
    kern = functools.partial(_inproj_kernel, absorbed=absorbed, q_lora=q_lora, kv_lora=kv_lora, d_cmlp=d_cmlp)
    return pl.pallas_call(
        kern, grid=grid, in_specs=in_specs, out_specs=out_specs, out_shape=out_shape,
        compiler_params=_params(("arbitrary",)),
        name="inproj_sample" if absorbed else "inproj_prompt",
    )(*args)


def _flash_kernel(q_ref, k_ref, v_ref, o_ref):
    tq = q_ref.shape[0]
    t = k_ref.shape[0]
    qi = pl.program_id(2)
    row = lax.broadcasted_iota(jnp.int32, (tq, tq), 0)
    col = lax.broadcasted_iota(jnp.int32, (tq, tq), 1)

    def one_head(hh, n_past):
        qk = slice(hh * D_QK, (hh + 1) * D_QK)
        vo = slice(hh * D_V, (hh + 1) * D_V)
        diag = pl.ds(n_past, tq)
        q = q_ref[:, qk]
        s_d = jnp.where(col <= row, _dot_nt(q, k_ref[diag, qk]), NEG)
        m = jnp.max(s_d, axis=-1, keepdims=True)
        if n_past:
            s_p = _dot_nt(q, k_ref[0:n_past, qk])
            m = jnp.maximum(m, jnp.max(s_p, axis=-1, keepdims=True))
            p_p = jnp.exp(s_p - m)
        p_d = jnp.exp(s_d - m)
        l = jnp.sum(p_d, axis=-1, keepdims=True)
        o = _dot(p_d.astype(BF16), v_ref[diag, vo])
        if n_past:
            l = l + jnp.sum(p_p, axis=-1, keepdims=True)
            o = o + _dot(p_p.astype(BF16), v_ref[0:n_past, vo])
        o_ref[:, vo] = (o / l).astype(o_ref.dtype)

    for c in range(t // tq):
        @pl.when(qi == c)
        def _(n_past=c * tq):
            for hh in range(FLASH_HEADS):
                one_head(hh, n_past)


FLASH_HEADS = 4


def _flash(q_cat, k_cat, v_h, b, t):
    tq = _largest_tile(t, 512, 16)
    nq = t // tq
    hp = FLASH_HEADS
    return pl.pallas_call(
        _flash_kernel,
        grid=(b, N_HEADS // hp, nq),
        in_specs=[
            pl.BlockSpec((tq, hp * D_QK), lambda bi, h, i: (bi * nq + i, h)),
            pl.BlockSpec((t, hp * D_QK), lambda bi, h, i: (bi, h)),
            pl.BlockSpec((t, hp * D_V), lambda bi, h, i: (bi, h)),
        ],
        out_specs=pl.BlockSpec((tq, hp * D_V), lambda bi, h, i: (bi * nq + i, h)),
        out_shape=jax.ShapeDtypeStruct((b * t, N_HEADS * D_V), BF16),
        compiler_params=_params(("arbitrary", "arbitrary", "arbitrary")),
        name="flash_prompt",
    )(q_cat, k_cat, v_h)


PAGE_RING = 4


def _paged_kernel(pt_ref, q_ref, ckv_ref, kr_ref, lat_hbm, krp_hbm, o_ref, lat_buf, krp_buf, sem,
                  *, layer, n_group, n_steps, kv_lora):
    b = pl.program_id(0)
    nb = pl.num_programs(0)
    s_new = ckv_ref.shape[1]
    n_rows = N_HEADS * s_new

    def page_copies(bi, g):
        slot = g % PAGE_RING
        copies = []
        for k in range(n_group):
            page = pt_ref[bi, g * n_group + k]
            rows = pl.ds(k * PAGE_SIZE, PAGE_SIZE)
            copies.append(pltpu.make_async_copy(lat_hbm.at[layer, page], lat_buf.at[slot, rows, :], sem.at[0, slot]))
            copies.append(pltpu.make_async_copy(krp_hbm.at[layer, page], krp_buf.at[slot, :, rows], sem.at[1, slot]))
        return copies

    def start_group(g):
        if g < n_steps:
            for cp in page_copies(b, g):
                cp.start()
        else:
            @pl.when(b + 1 < nb)
            def _():
                for cp in page_copies(b + 1, g - n_steps):
                    cp.start()

    @pl.when(b == 0)
    def _():
        for g in range(PAGE_RING - 1):
            start_group(g)

    q = q_ref[0].reshape(n_rows, kv_lora + LANES)
    q_lat = q[:, :kv_lora].astype(BF16)
    q_rope = q[:, kv_lora:kv_lora + D_ROPE].astype(BF16)

    def scores(g):
        slot = g % PAGE_RING
        return _dot_nt(q_lat, lat_buf[slot].astype(BF16)) + _dot(q_rope, krp_buf[slot].astype(BF16))

    def attend(state, s, kv_bf):
        m_old, l_old, acc = state
        m_new = jnp.maximum(m_old, jnp.max(s, axis=-1, keepdims=True))
        corr = jnp.exp(m_old - m_new)
        p = jnp.exp(s - m_new)
        l_new = l_old * corr + jnp.sum(p, axis=-1, keepdims=True)
        return m_new, l_new, acc * corr + _dot(p.astype(BF16), kv_bf)

    pad = PAGE_SIZE - s_new
    kv0 = jnp.concatenate([ckv_ref[0], jnp.zeros((pad, kv_lora), F32)], axis=0).astype(BF16)
    kr0 = jnp.concatenate([kr_ref[0], jnp.zeros((pad, D_ROPE), F32)], axis=0).astype(BF16)
    s0 = _dot_nt(q_lat, kv0) + _dot_nt(q_rope, kr0)
    qpos = lax.broadcasted_iota(jnp.int32, (n_rows, PAGE_SIZE), 0) % s_new
    kpos = lax.broadcasted_iota(jnp.int32, (n_rows, PAGE_SIZE), 1)
    s0 = jnp.where(kpos <= qpos, s0, NEG)
    state = (jnp.full((n_rows, 1), NEG, F32), jnp.zeros((n_rows, 1), F32), jnp.zeros((n_rows, kv_lora), F32))
    state = attend(state, s0, kv0)

    for cp in page_copies(b, 0):
        cp.wait()
    s = scores(0)
    for g in range(n_steps):
        start_group(g + PAGE_RING - 1)
        if g + 1 < n_steps:
            for cp in page_copies(b, g + 1):
                cp.wait()
            s_next = scores(g + 1)
        state = attend(state, s, lat_buf[g % PAGE_RING].astype(BF16))
        if g + 1 < n_steps:
            s = s_next

    _, l_fin, acc = state
    o_ref[0] = (acc / l_fin).reshape(N_HEADS, s_new, kv_lora)


def _paged(q_cat, ckv_new, kr_new, cache_latent, cache_krope_t, page_table, layer):
    b, _, s_new, d_q = q_cat.shape
    kv_lora = ckv_new.shape[-1]
    n_pages = page_table.shape[1]
    assert n_pages % PAGE_RING == 0
    n_group = _largest_tile(n_pages // PAGE_RING, 16, 1)
    n_steps = n_pages // n_group
    grid_spec = pltpu.PrefetchScalarGridSpec(
        num_scalar_prefetch=1,
        grid=(b,),
        in_specs=[
            pl.BlockSpec((1, N_HEADS, s_new, d_q), lambda bi, pt: (bi, 0, 0, 0)),
            pl.BlockSpec((1, s_new, kv_lora), lambda bi, pt: (bi, 0, 0)),
            pl.BlockSpec((1, s_new, D_ROPE), lambda bi, pt: (bi, 0, 0)),
            pl.BlockSpec(memory_space=pl.ANY),
            pl.BlockSpec(memory_space=pl.ANY),
        ],
        out_specs=pl.BlockSpec((1, N_HEADS, s_new, kv_lora), lambda bi, pt: (bi, 0, 0, 0)),
        scratch_shapes=[
            pltpu.VMEM((PAGE_RING, n_group * PAGE_SIZE, kv_lora), F32),
            pltpu.VMEM((PAGE_RING, D_ROPE, n_group * PAGE_SIZE), F32),
            pltpu.SemaphoreType.DMA((2, PAGE_RING)),
        ],
    )
    kern = functools.partial(_paged_kernel, layer=layer, n_group=n_group, n_steps=n_steps, kv_lora=kv_lora)
    return pl.pallas_call(
        kern, grid_spec=grid_spec,
        out_shape=jax.ShapeDtypeStruct((b, N_HEADS, s_new, kv_lora), F32),
        compiler_params=_params(("arbitrary",)),
        name="paged_sample",
    )(page_table, q_cat, ckv_new.reshape(b, s_new, kv_lora), kr_new.reshape(b, s_new, D_ROPE),
      cache_latent, cache_krope_t)


def _mix_kernel(*refs, absorbed, alpha, run, d_att):
    if absorbed:
        (x_ref, g1_ref, sc2_ref, sh2_ref, att_ref, wuv_ref, u_ref, v_ref, ws_ref, bs_ref, wout_ref, lng_ref, lnb_ref,
         x1_out, h2_out, cm_scr, att_scr) = refs
    else:
        (x_ref, g1_ref, sc2_ref, sh2_ref, att_ref, u_ref, v_ref, ws_ref, bs_ref, wout_ref, lng_ref, lnb_ref,
         x1_out, h2_out, cm_scr) = refs
    bb, tt, d = x_ref.shape
    tm = bb * tt
    mb = bs_ref.shape[0]
    d_cmlp = u_ref.shape[1]
    d_head = d_cmlp // N_CMLP_HEADS

    if absorbed:
        for hd in range(N_HEADS):
            o_lat = att_ref[:, hd].reshape(tm, att_ref.shape[-1]).astype(BF16)
            att_scr[:, hd * D_V:(hd + 1) * D_V] = _dot(o_lat, wuv_ref[hd]).astype(BF16)
        mix = _dot(att_scr[...], wout_ref[0:d_att, :])
    else:
        mix = _dot(att_ref[...], wout_ref[0:d_att, :])

    r = lax.broadcasted_iota(jnp.int32, (mb, mb), 0)
    c = lax.broadcasted_iota(jnp.int32, (mb, mb), 1)
    shift = run.bit_length() - 1
    keep = ((r >> shift) == (c >> shift)) & ((c & (run - 1)) <= (r & (run - 1)))
    bs = bs_ref[...]
    for hd in range(N_CMLP_HEADS):
        w_rows = ws_ref[hd]
        w_blk = jnp.concatenate([w_rows] * (mb // w_rows.shape[0]), axis=0)
        w_blk = jnp.concatenate([w_blk] * (mb // w_rows.shape[1]), axis=1)
        wm = jnp.where(keep, w_blk, 0.0).astype(BF16)
        bias = bs[:, hd:hd + 1]
        cols = slice(hd * d_head, (hd + 1) * d_head)
        for blk in range(tm // mb):
            rows = slice(blk * mb, (blk + 1) * mb)
            mixed = _dot(wm, v_ref[rows, cols].astype(BF16)) + bias
            cm_scr[rows, cols] = (u_ref[rows, cols] * mixed).astype(BF16)

    mix = mix + _dot(cm_scr[...], wout_ref[d_att:d_att + d_cmlp, :])
    y = alpha * x_ref[...] + g1_ref[...] * mix.reshape(bb, tt, d)
    x1 = _layernorm(y, lng_ref[...], lnb_ref[...])
    x1_out[...] = x1
    h2 = x1 * (1.0 + sc2_ref[...]) + sh2_ref[...]
    h2_out[...] = h2.reshape(tm, d).astype(BF16)


def _mix(x, g1, sc2, sh2, att, u_act, v_ln, w, *, absorbed, alpha, tm_target):
    b, t, d = x.shape
    d_cmlp = u_act.shape[1]
    d_att = N_HEADS * D_V
    run = min(t, CHUNK)
    assert run & (run - 1) == 0 and t % run == 0
    if absorbed:
        tt, bb = t, _largest_tile(b, max(1, tm_target // t), 1)
    else:
        tt, bb = _largest_tile(t, tm_target, CHUNK), 1
    nt = t // tt
    tm = bb * tt
    mb = w["bs_rep"].shape[0]
    assert tm % mb == 0
    n_tok = b * t
    row3 = lambda i: (i // nt, i % nt, 0)
    mod3 = lambda i: (i // nt, 0, 0)
    row2 = lambda i: (i, 0)
    in_specs = [
        pl.BlockSpec((bb, tt, d), row3),
        pl.BlockSpec((bb, 1, d), mod3),
        pl.BlockSpec((bb, 1, d), mod3),
        pl.BlockSpec((bb, 1, d), mod3),
    ]
    args = [x, g1, sc2, sh2]
    if absorbed:
        in_specs += [pl.BlockSpec((bb, N_HEADS, t, att.shape[-1]), lambda i: (i, 0, 0, 0)),
                     _const_spec(w["w_uv_h"].shape)]
        args += [att, w["w_uv_h"]]
    else:
        in_specs.append(pl.BlockSpec((tm, d_att), row2))
        args.append(att)
    in_specs += [
        pl.BlockSpec((tm, d_cmlp), row2),
        pl.BlockSpec((tm, d_cmlp), row2),
        _const_spec(w["ws_rep"].shape),
        _const_spec(w["bs_rep"].shape),
        _const_spec(w["w_out"].shape),
        _const_spec((1, d)),
        _const_spec((1, d)),
    ]
    args += [u_act, v_ln, w["ws_rep"], w["bs_rep"], w["w_out"], w["ln1_g"], w["ln1_b"]]
    kern = functools.partial(_mix_kernel, absorbed=absorbed, alpha=alpha, run=run, d_att=d_att)
    return pl.pallas_call(
        kern, grid=(n_tok // tm,), in_specs=in_specs,
        out_specs=[pl.BlockSpec((bb, tt, d), row3), pl.BlockSpec((tm, d), row2)],
        out_shape=[jax.ShapeDtypeStruct((b, t, d), F32), jax.ShapeDtypeStruct((n_tok, d), BF16)],
        scratch_shapes=[pltpu.VMEM((tm, d_cmlp), BF16)] + ([pltpu.VMEM((tm, d_att), BF16)] if absorbed else []),
        compiler_params=_params(("arbitrary",)),
        name="mix_sample" if absorbed else "mix_prompt",
    )(*args)


def _ffn_kernel(h_ref, x1_ref, g2_ref, wg_ref, wu_ref, wd_ref, lng_ref, lnb_ref, o_ref, acc_ref, *, alpha):
    f = pl.program_id(1)
    bb, tt, d = x1_ref.shape
    @pl.when(f == 0)
    def _():
        acc_ref[...] = jnp.zeros(acc_ref.shape, F32)

    h = h_ref[...]
    a = (jax.nn.silu(_dot(h, wg_ref[...])) * _dot(h, wu_ref[...])).astype(BF16)
    acc_ref[...] += _dot(a, wd_ref[...])

    @pl.when(f == pl.num_programs(1) - 1)
    def _():
        y = alpha * x1_ref[...] + g2_ref[...] * acc_ref[...].reshape(bb, tt, d)
        o_ref[...] = _layernorm(y, lng_ref[...], lnb_ref[...])


def _ffn(h2, x1, g2, w, *, absorbed, alpha, tm_target, tf_target):
    b, t, d = x1.shape
    d_ff = w["w_gate"].shape[1]
    if absorbed:
        tt, bb = t, _largest_tile(b, max(1, tm_target // t), 1)
    else:
        tt, bb = _largest_tile(t, tm_target, 16), 1
    nt = t // tt
    tm = bb * tt
    tf = _largest_tile(d_ff, tf_target, LANES)
    row3 = lambda i, f: (i // nt, i % nt, 0)
    return pl.pallas_call(
        functools.partial(_ffn_kernel, alpha=alpha),
        grid=(b * t // tm, d_ff // tf),
        in_specs=[
            pl.BlockSpec((tm, d), lambda i, f: (i, 0)),
            pl.BlockSpec((bb, tt, d), row3),
            pl.BlockSpec((bb, 1, d), lambda i, f: (i // nt, 0, 0)),
            pl.BlockSpec((d, tf), lambda i, f: (0, f)),
            pl.BlockSpec((d, tf), lambda i, f: (0, f)),
            pl.BlockSpec((tf, d), lambda i, f: (f, 0)),
            _const_spec((1, d)),
            _const_spec((1, d)),
        ],
        out_specs=pl.BlockSpec((bb, tt, d), row3),
        out_shape=jax.ShapeDtypeStruct((b, t, d), F32),
        scratch_shapes=[pltpu.VMEM((tm, d), F32)],
        compiler_params=_params(("arbitrary", "arbitrary")),
        name="ffn_sample" if absorbed else "ffn_prompt",
    )(h2, x1, g2, w["w_gate"], w["w_up"], w["w_down"], w["ln2_g"], w["ln2_b"])


def _rope_pair(a):
    half = D_ROPE // 2
    return jnp.concatenate([a, -a[..., half:], a[..., :half]], axis=-1)


def _layer_weights(l, w_in, q_norm, w_uq, kv_norm, w_uk, w_uv, v_ln_g, v_ln_b, w_s, b_s, w_out, ln1_g, ln1_b,
                   w_gate, w_up, w_down, ln2_g, ln2_b, t_prompt, t_sample, mb_sample):
    q_lora, kv_lora, d_cmlp = q_norm.shape[-1], kv_norm.shape[-1], v_ln_g.shape[-1]
    wi = w_in[l]
    i1 = q_lora + kv_lora
    i2 = i1 + D_ROPE
    wq = w_uq[l].reshape(q_lora, N_HEADS, D_NOPE + D_ROPE)
    w_q = jnp.concatenate([wq[..., :D_NOPE].reshape(q_lora, -1),
                           _rope_pair(wq[..., D_NOPE:]).reshape(q_lora, -1)], axis=1).astype(BF16)
    row = lambda a: a[l].reshape(1, -1)

    def mixing(t, mb):
        run = min(t, CHUNK)
        return (jnp.tile(w_s[l][:, :run, :run], (1, 1, max(1, min(LANES, mb) // run))),
                jnp.tile(b_s[l][:, :run], (1, mb // run)).T)

    ws_p, bs_p = mixing(t_prompt, min(t_prompt, CHUNK))
    ws_s, bs_s = mixing(t_sample, mb_sample)
    common = {
        "w_lat": wi[:, :i1].astype(BF16), "w_uv_in": wi[:, i2:].astype(BF16),
        "w_kr": _rope_pair(wi[:, i1:i2]).astype(BF16),
        "w_q": w_q, "q_norm": row(q_norm), "kv_norm": row(kv_norm),
        "v_ln_g": row(v_ln_g), "v_ln_b": row(v_ln_b), "w_out": w_out[l].astype(BF16),
        "ln1_g": row(ln1_g), "ln1_b": row(ln1_b), "w_gate": w_gate[l].astype(BF16), "w_up": w_up[l].astype(BF16),
        "w_down": w_down[l].astype(BF16), "ln2_g": row(ln2_g), "ln2_b": row(ln2_b),
    }
    prompt = dict(common, w_uk_f=w_uk[l].reshape(kv_lora, -1).astype(BF16),
                  w_uv_f=w_uv[l].reshape(kv_lora, -1).astype(BF16), ws_rep=ws_p, bs_rep=bs_p)
    sample = dict(common, w_uk_t=jnp.transpose(w_uk[l], (1, 2, 0)).astype(BF16),
                  w_uv_h=jnp.transpose(w_uv[l], (1, 0, 2)).astype(BF16), ws_rep=ws_s, bs_rep=bs_s)
    return prompt, sample


def _rope_tables(pos):
    inv = ROPE_THETA ** (-jnp.arange(0, D_ROPE, 2, dtype=F32) / D_ROPE)
    ang = pos[:, None] * inv[None, :]
    z = jnp.zeros((pos.shape[0], D_ROPE), F32)
    cos, sin = jnp.cos(ang), jnp.sin(ang)
    return jnp.concatenate([cos, cos, z], axis=1), jnp.concatenate([sin, sin, z], axis=1)


def _mods(mod):
    return [m[:, None, :] for m in jnp.split(mod, 6, axis=-1)]


def kernel(x_prompt, x_sample, cache_latent, cache_krope, page_table, c_prompt, c_sample, w_ada, b_ada, w_in, q_norm,
           w_uq, kv_norm, w_uk, w_uv, v_ln_g, v_ln_b, w_s, b_s, w_out, ln1_g, ln1_b, w_gate, w_up, w_down, ln2_g,
           ln2_b):
    depth = w_in.shape[0]
    alpha = (2.0 * depth) ** 0.25
    bp, tp, d = x_prompt.shape
    bs, ts, _ = x_sample.shape
    past_len = page_table.shape[1] * PAGE_SIZE
    kv_lora = kv_norm.shape[-1]
    d_cmlp = v_ln_g.shape[-1]
    cdt = cache_latent.dtype

    xp = x_prompt.astype(F32)
    xs = x_sample.astype(F32)
    cos_p, sin_p = _rope_tables(jnp.arange(tp, dtype=F32))
    cos_s, sin_s = _rope_tables(past_len + jnp.arange(ts, dtype=F32))
    c_all = jnp.concatenate([c_prompt, c_sample], axis=0).astype(F32)
    krope_t = jnp.swapaxes(cache_krope, 2, 3)
    tm_s = _largest_tile(bs, max(1, 256 // ts), 1) * ts

    lat_p, kr_p, v_p, lat_s, kr_s, v_s = [], [], [], [], [], []
    for l in range(depth):
        wp, ws = _layer_weights(l, w_in, q_norm, w_uq, kv_norm, w_uk, w_uv, v_ln_g, v_ln_b, w_s, b_s, w_out, ln1_g,
                                ln1_b, w_gate, w_up, w_down, ln2_g, ln2_b, tp, ts, tm_s)
        mod = _ada(c_all, w_ada[l], b_ada[l])
        sh1, sc1, g1, sh2, sc2, g2 = _mods(mod[:bp])
        ckv, kr, u_act, v_ln, q_cat, k_cat, v_h = _inproj(xp, sc1, sh1, cos_p, sin_p, wp, absorbed=False,
                                                          tm_target=256)
        att = _flash(q_cat, k_cat, v_h, bp, tp)
        x1, h2 = _mix(xp, g1, sc2, sh2, att, u_act, v_ln, wp, absorbed=False, alpha=alpha, tm_target=256)
        xp = _ffn(h2, x1, g2, wp, absorbed=False, alpha=alpha, tm_target=512, tf_target=512)
        lat_p.append(ckv.reshape(bp, tp, kv_lora))
        kr_p.append(kr.reshape(bp, tp, D_ROPE))
        v_p.append(v_ln.reshape(bp, tp, d_cmlp)[:, tp - min(tp, CHUNK):])

        sh1, sc1, g1, sh2, sc2, g2 = _mods(mod[bp:])
        ckv, kr, u_act, v_ln, q_cat = _inproj(xs, sc1, sh1, cos_s, sin_s, ws, absorbed=True, tm_target=tm_s)
        o_lat = _paged(q_cat, ckv, kr, cache_latent, krope_t, page_table, l)
        x1, h2 = _mix(xs, g1, sc2, sh2, o_lat, u_act, v_ln, ws, absorbed=True, alpha=alpha, tm_target=tm_s)
        xs = _ffn(h2, x1, g2, ws, absorbed=True, alpha=alpha, tm_target=512, tf_target=512)
        lat_s.append(ckv.reshape(bs, ts, kv_lora))
        kr_s.append(kr.reshape(bs, ts, D_ROPE))
        v_s.append(v_ln.reshape(bs, ts, d_cmlp))

    return (xp.astype(x_prompt.dtype), xs.astype(x_sample.dtype),
            jnp.stack(lat_p).astype(cdt), jnp.stack(kr_p).astype(cdt), jnp.stack(v_p).astype(cdt),
            jnp.stack(lat_s).astype(cdt), jnp.stack(kr_s).astype(cdt), jnp.stack(v_s).astype(cdt))
```

```python
import functools
import math

import jax
import jax.numpy as jnp
from jax import lax
from jax.experimental import pallas as pl
from jax.experimental.pallas import tpu as pltpu

N_HEADS = 8
D_NOPE = 128
D_ROPE = 64
D_V = 128
N_CMLP_HEADS = 8
CHUNK = 128
PAGE_SIZE = 128
ROPE_THETA = 10000.0
RMS_EPS = 1e-6
LN_EPS = 1e-5
NEG = -1e30
SM_SCALE = 1.0 / math.sqrt(D_NOPE + D_ROPE)

LANES = 128
SUBLANES = 8
D_QK = 2 * LANES
VMEM_LIMIT = 56 * 1024 * 1024

F32 = jnp.float32
BF16 = jnp.bfloat16


def _largest_tile(n, target, multiple):
    if n <= target:
        return n
    t = (target // multiple) * multiple
    while t >= multiple:
        if n % t == 0:
            return t
        t -= multiple
    raise ValueError(f"no tile for {n}")


def _const_spec(shape):
    nd = len(shape)
    return pl.BlockSpec(shape, lambda *_: (0,) * nd, pipeline_mode=pl.Buffered(1))


def _params(semantics):
    return pltpu.CompilerParams(dimension_semantics=semantics, vmem_limit_bytes=VMEM_LIMIT)


def _layernorm(x, g, b):
    mu = jnp.mean(x, axis=-1, keepdims=True)
    xc = x - mu
    var = jnp.mean(xc * xc, axis=-1, keepdims=True)
    return xc * lax.rsqrt(var + LN_EPS) * g + b


def _rmsnorm(x, g):
    return x * lax.rsqrt(jnp.mean(x * x, axis=-1, keepdims=True) + RMS_EPS) * g


def _dot(a, b):
    return jnp.dot(a, b, preferred_element_type=F32)


def _dot_nt(a, b):
    return lax.dot_general(a, b, (((1,), (1,)), ((), ())), preferred_element_type=F32)


def _ada_kernel(c_ref, w_ref, b_ref, o_ref):
    c = c_ref[...]
    o_ref[...] = _dot(jax.nn.silu(c).astype(BF16), w_ref[...].astype(BF16)) + b_ref[...]


def _ada(c_all, w_ada, b_ada):
    n, d = c_all.shape
    dout = w_ada.shape[1]
    tn = _largest_tile(dout, 1536, LANES)
    return pl.pallas_call(
        _ada_kernel,
        grid=(dout // tn,),
        in_specs=[
            _const_spec((n, d)),
            pl.BlockSpec((d, tn), lambda j: (0, j)),
            pl.BlockSpec((1, tn), lambda j: (0, j)),
        ],
        out_specs=pl.BlockSpec((n, tn), lambda j: (0, j)),
        out_shape=jax.ShapeDtypeStruct((n, dout), F32),
        compiler_params=_params(("arbitrary",)),
        name="ada",
    )(c_all, w_ada, b_ada.reshape(1, dout))


def _rope_tile(t, cos, sin, bb, tt):
    rolled = pltpu.roll(t, D_ROPE, axis=1)
    r = t.reshape(bb, tt, LANES) * cos[None] + rolled.reshape(bb, tt, LANES) * sin[None]
    return r.reshape(bb * tt, LANES)


def _inproj_kernel(*refs, absorbed, q_lora, kv_lora, d_cmlp):
    if absorbed:
        (x_ref, sc_ref, sh_ref, cos_ref, sin_ref, wlat_ref, wuv_ref, wkr_ref, qn_ref, kvn_ref, vg_ref, vb_ref,
         wq_ref, wk_ref, ckv_out, kr_out, u_out, v_out, q_out) = refs
    else:
        (x_ref, sc_ref, sh_ref, cos_ref, sin_ref, wlat_ref, wuv_ref, wkr_ref, qn_ref, kvn_ref, vg_ref, vb_ref,
         wq_ref, wk_ref, wv_ref, ckv_out, kr_out, u_out, v_out, q_out, k_out, vh_out) = refs
    bb, tt, d = x_ref.shape
    tm = bb * tt
    cos = cos_ref[...]
    sin = sin_ref[...]

    h = x_ref[...] * (1.0 + sc_ref[...]) + sh_ref[...]
    h = h.reshape(tm, d).astype(BF16)

    ckv = _rmsnorm(_dot(h, wlat_ref[:, q_lora:q_lora + kv_lora]), kvn_ref[...])
    ckv_out[...] = ckv
    ckv_bf = ckv.astype(BF16)

    u_out[...] = jax.nn.gelu(_dot(h, wuv_ref[:, 0:d_cmlp]))
    v_out[...] = _layernorm(jax.nn.gelu(_dot(h, wuv_ref[:, d_cmlp:2 * d_cmlp])), vg_ref[...], vb_ref[...])

    kro = _rope_tile(_dot(h, wkr_ref[...]), cos, sin, bb, tt)
    kr_out[...] = kro[:, :D_ROPE]

    cq = _rmsnorm(_dot(h, wlat_ref[:, 0:q_lora]), qn_ref[...]).astype(BF16)
    n_nope = N_HEADS * D_NOPE
    q_nope = _dot(cq, wq_ref[:, 0:n_nope])
    q_rt = _dot(cq, wq_ref[:, n_nope:n_nope + N_HEADS * LANES])

    if absorbed:
        for hd in range(N_HEADS):
            qn_h = q_nope[:, hd * D_NOPE:(hd + 1) * D_NOPE].astype(BF16)
            q_lat = _dot(qn_h, wk_ref[hd]) * SM_SCALE
            r_h = _rope_tile(q_rt[:, hd * LANES:(hd + 1) * LANES], cos, sin, bb, tt) * SM_SCALE
            q_out[:, hd, :, 0:kv_lora] = q_lat.reshape(bb, tt, kv_lora)
            q_out[:, hd, :, kv_lora:kv_lora + LANES] = r_h.reshape(bb, tt, LANES)
    else:
        k_nope = _dot(ckv_bf, wk_ref[...])
        vh_out[...] = _dot(ckv_bf, wv_ref[...]).astype(BF16)
        kro_bf = kro.astype(BF16)
        for hd in range(N_HEADS):
            r_h = _rope_tile(q_rt[:, hd * LANES:(hd + 1) * LANES], cos, sin, bb, tt) * SM_SCALE
            q_out[:, hd * D_QK:hd * D_QK + D_NOPE] = (q_nope[:, hd * D_NOPE:(hd + 1) * D_NOPE] * SM_SCALE).astype(BF16)
            q_out[:, hd * D_QK + D_NOPE:(hd + 1) * D_QK] = r_h.astype(BF16)
            k_out[:, hd * D_QK:hd * D_QK + D_NOPE] = k_nope[:, hd * D_NOPE:(hd + 1) * D_NOPE].astype(BF16)
            k_out[:, hd * D_QK + D_NOPE:(hd + 1) * D_QK] = kro_bf


def _inproj(x, sc, sh, cos_t, sin_t, w, *, absorbed, tm_target):
    b, t, d = x.shape
    q_lora, kv_lora, d_cmlp = w["q_norm"].shape[-1], w["kv_norm"].shape[-1], w["v_ln_g"].shape[-1]
    if absorbed:
        tt, bb = t, _largest_tile(b, max(1, tm_target // t), 1)
    else:
        tt, bb = _largest_tile(t, tm_target, 16), 1
    nt = t // tt
    tm = bb * tt
    n_tok = b * t
    grid = (n_tok // tm,)

    row3 = lambda i: (i // nt, i % nt, 0)
    mod3 = lambda i: (i // nt, 0, 0)
    row2 = lambda i: (i, 0)
    in_specs = [
        pl.BlockSpec((bb, tt, d), row3),
        pl.BlockSpec((bb, 1, d), mod3),
        pl.BlockSpec((bb, 1, d), mod3),
        pl.BlockSpec((tt, LANES), lambda i: (i % nt, 0)),
        pl.BlockSpec((tt, LANES), lambda i: (i % nt, 0)),
        _const_spec(w["w_lat"].shape),
        _const_spec(w["w_uv_in"].shape),
        _const_spec(w["w_kr"].shape),
        _const_spec((1, q_lora)),
        _const_spec((1, kv_lora)),
        _const_spec((1, d_cmlp)),
        _const_spec((1, d_cmlp)),
        _const_spec(w["w_q"].shape),
    ]
    args = [x, sc, sh, cos_t, sin_t, w["w_lat"], w["w_uv_in"], w["w_kr"], w["q_norm"], w["kv_norm"], w["v_ln_g"],
            w["v_ln_b"], w["w_q"]]
    out_specs = [
        pl.BlockSpec((tm, kv_lora), row2),
        pl.BlockSpec((tm, D_ROPE), row2),
        pl.BlockSpec((tm, d_cmlp), row2),
        pl.BlockSpec((tm, d_cmlp), row2),
    ]
    out_shape = [
        jax.ShapeDtypeStruct((n_tok, kv_lora), F32),
        jax.ShapeDtypeStruct((n_tok, D_ROPE), F32),
        jax.ShapeDtypeStruct((n_tok, d_cmlp), F32),
        jax.ShapeDtypeStruct((n_tok, d_cmlp), F32),
    ]
    if absorbed:
        in_specs.append(_const_spec(w["w_uk_t"].shape))
        args.append(w["w_uk_t"])
        d_q = kv_lora + LANES
        out_specs.append(pl.BlockSpec((bb, N_HEADS, t, d_q), lambda i: (i, 0, 0, 0)))
        out_shape.append(jax.ShapeDtypeStruct((b, N_HEADS, t, d_q), F32))
    else:
        in_specs += [_const_spec(w["w_uk_f"].shape), _const_spec(w["w_uv_f"].shape)]
        args += [w["w_uk_f"], w["w_uv_f"]]
        out_specs += [
            pl.BlockSpec((tm, N_HEADS * D_QK), row2),
            pl.BlockSpec((tm, N_HEADS * D_QK), row2),
            pl.BlockSpec((tm, N_HEADS * D_V), row2),
        ]
        out_shape += [
            jax.ShapeDtypeStruct((n_tok, N_HEADS * D_QK), BF16),
            jax.ShapeDtypeStruct((n_tok, N_HEADS * D_QK), BF16),
            jax.ShapeDtypeStruct((n_tok, N_HEADS * D_V), BF16),
        ]
    kern = functools.partial(_inproj_kernel, absorbed=absorbed, q_lora=q_lora, kv_lora=kv_lora, d_cmlp=d_cmlp)
    return pl.pallas_call(
        kern, grid=grid, in_specs=in_specs, out_specs=out_specs, out_shape=out_shape,
        compiler_params=_params(("arbitrary",)),
        name="inproj_sample" if absorbed else "inproj_prompt",
    )(*args)


def _flash_kernel(q_ref, k_ref, v_ref, o_ref):
    tq = q_ref.shape[0]
    t = k_ref.shape[0]
    qi = pl.program_id(2)
    row = lax.broadcasted_iota(jnp.int32, (tq, tq), 0)
    col = lax.broadcasted_iota(jnp.int32, (tq, tq), 1)

    def one_head(hh, n_past):
        qk = slice(hh * D_QK, (hh + 1) * D_QK)
        vo = slice(hh * D_V, (hh + 1) * D_V)
        diag = pl.ds(n_past, tq)
        q = q_ref[:, qk]
        s_d = jnp.where(col <= row, _dot_nt(q, k_ref[diag, qk]), NEG)
        m = jnp.max(s_d, axis=-1, keepdims=True)
        if n_past:
            s_p = _dot_nt(q, k_ref[0:n_past, qk])
            m = jnp.maximum(m, jnp.max(s_p, axis=-1, keepdims=True))
            p_p = jnp.exp(s_p - m)
        p_d = jnp.exp(s_d - m)
        l = jnp.sum(p_d, axis=-1, keepdims=True)
        o = _dot(p_d.astype(BF16), v_ref[diag, vo])
        if n_past:
            l = l + jnp.sum(p_p, axis=-1, keepdims=True)
            o = o + _dot(p_p.astype(BF16), v_ref[0:n_past, vo])
        o_ref[:, vo] = (o / l).astype(o_ref.dtype)

    for c in range(t // tq):
        @pl.when(qi == c)
        def _(n_past=c * tq):
            for hh in range(FLASH_HEADS):
                one_head(hh, n_past)


FLASH_HEADS = 4


def _flash(q_cat, k_cat, v_h, b, t):
    tq = _largest_tile(t, 512, 16)
    nq = t // tq
    hp = FLASH_HEADS
    return pl.pallas_call(
        _flash_kernel,
        grid=(b, N_HEADS // hp, nq),
        in_specs=[
            pl.BlockSpec((tq, hp * D_QK), lambda bi, h, i: (bi * nq + i, h)),
            pl.BlockSpec((t, hp * D_QK), lambda bi, h, i: (bi, h)),
            pl.BlockSpec((t, hp * D_V), lambda bi, h, i: (bi, h)),
        ],
        out_specs=pl.BlockSpec((tq, hp * D_V), lambda bi, h, i: (bi * nq + i, h)),
        out_shape=jax.ShapeDtypeStruct((b * t, N_HEADS * D_V), BF16),
        compiler_params=_params(("arbitrary", "arbitrary", "arbitrary")),
        name="flash_prompt",
    )(q_cat, k_cat, v_h)


PAGE_RING = 4


def _paged_kernel(pt_ref, q_ref, ckv_ref, kr_ref, lat_hbm, krp_hbm, o_ref, lat_buf, krp_buf, sem,
                  *, layer, n_group, n_steps, kv_lora):
    b = pl.program_id(0)
    nb = pl.num_programs(0)
    s_new = ckv_ref.shape[1]
    n_rows = N_HEADS * s_new

    def page_copies(bi, g):
        slot = g % PAGE_RING
        copies = []
        for k in range(n_group):
            page = pt_ref[bi, g * n_group + k]
            rows = pl.ds(k * PAGE_SIZE, PAGE_SIZE)
            copies.append(pltpu.make_async_copy(lat_hbm.at[layer, page], lat_buf.at[slot, rows, :], sem.at[0, slot]))
            copies.append(pltpu.make_async_copy(krp_hbm.at[layer, page], krp_buf.at[slot, :, rows], sem.at[1, slot]))
        return copies

    def start_group(g):
        if g < n_steps:
            for cp in page_copies(b, g):
                cp.start()
        else:
            @pl.when(b + 1 < nb)
            def _():
                for cp in page_copies(b + 1, g - n_steps):
                    cp.start()

    @pl.when(b == 0)
    def _():
        for g in range(PAGE_RING - 1):
            start_group(g)

    q = q_ref[0].reshape(n_rows, kv_lora + LANES)
    q_lat = q[:, :kv_lora].astype(BF16)
    q_rope = q[:, kv_lora:kv_lora + D_ROPE].astype(BF16)

    def scores(g):
        slot = g % PAGE_RING
        return _dot_nt(q_lat, lat_buf[slot].astype(BF16)) + _dot(q_rope, krp_buf[slot].astype(BF16))

    def attend(state, s, kv_bf):
        m_old, l_old, acc = state
        m_new = jnp.maximum(m_old, jnp.max(s, axis=-1, keepdims=True))
        corr = jnp.exp(m_old - m_new)
        p = jnp.exp(s - m_new)
        l_new = l_old * corr + jnp.sum(p, axis=-1, keepdims=True)
        return m_new, l_new, acc * corr + _dot(p.astype(BF16), kv_bf)

    pad = PAGE_SIZE - s_new
    kv0 = jnp.concatenate([ckv_ref[0], jnp.zeros((pad, kv_lora), F32)], axis=0).astype(BF16)
    kr0 = jnp.concatenate([kr_ref[0], jnp.zeros((pad, D_ROPE), F32)], axis=0).astype(BF16)
    s0 = _dot_nt(q_lat, kv0) + _dot_nt(q_rope, kr0)
    qpos = lax.broadcasted_iota(jnp.int32, (n_rows, PAGE_SIZE), 0) % s_new
    kpos = lax.broadcasted_iota(jnp.int32, (n_rows, PAGE_SIZE), 1)
    s0 = jnp.where(kpos <= qpos, s0, NEG)
    state = (jnp.full((n_rows, 1), NEG, F32), jnp.zeros((n_rows, 1), F32), jnp.zeros((n_rows, kv_lora), F32))
    state = attend(state, s0, kv0)

    for cp in page_copies(b, 0):
        cp.wait()
    s = scores(0)
    for g in range(n_steps):
        start_group(g + PAGE_RING - 1)
        if g + 1 < n_steps:
            for cp in page_copies(b, g + 1):
                cp.wait()
            s_next = scores(g + 1)
        state = attend(state, s, lat_buf[g % PAGE_RING].astype(BF16))
        if g + 1 < n_steps:
            s = s_next

    _, l_fin, acc = state
    o_ref[0] = (acc / l_fin).reshape(N_HEADS, s_new, kv_lora)


def _paged(q_cat, ckv_new, kr_new, cache_latent, cache_krope_t, page_table, layer):
    b, _, s_new, d_q = q_cat.shape
    kv_lora = ckv_new.shape[-1]
    n_pages = page_table.shape[1]
    assert n_pages % PAGE_RING == 0
    n_group = _largest_tile(n_pages // PAGE_RING, 16, 1)
    n_steps = n_pages // n_group
    grid_spec = pltpu.PrefetchScalarGridSpec(
        num_scalar_prefetch=1,
        grid=(b,),
        in_specs=[
            pl.BlockSpec((1, N_HEADS, s_new, d_q), lambda bi, pt: (bi, 0, 0, 0)),
            pl.BlockSpec((1, s_new, kv_lora), lambda bi, pt: (bi, 0, 0)),
            pl.BlockSpec((1, s_new, D_ROPE), lambda bi, pt: (bi, 0, 0)),
            pl.BlockSpec(memory_space=pl.ANY),
            pl.BlockSpec(memory_space=pl.ANY),
        ],
        out_specs=pl.BlockSpec((1, N_HEADS, s_new, kv_lora), lambda bi, pt: (bi, 0, 0, 0)),
        scratch_shapes=[
            pltpu.VMEM((PAGE_RING, n_group * PAGE_SIZE, kv_lora), F32),
            pltpu.VMEM((PAGE_RING, D_ROPE, n_group * PAGE_SIZE), F32),
            pltpu.SemaphoreType.DMA((2, PAGE_RING)),
        ],
    )
    kern = functools.partial(_paged_kernel, layer=layer, n_group=n_group, n_steps=n_steps, kv_lora=kv_lora)
    return pl.pallas_call(
        kern, grid_spec=grid_spec,
        out_shape=jax.ShapeDtypeStruct((b, N_HEADS, s_new, kv_lora), F32),
        compiler_params=_params(("arbitrary",)),
        name="paged_sample",
    )(page_table, q_cat, ckv_new.reshape(b, s_new, kv_lora), kr_new.reshape(b, s_new, D_ROPE),
      cache_latent, cache_krope_t)


def _mix_kernel(*refs, absorbed, alpha, run, d_att):
    if absorbed:
        (x_ref, g1_ref, sc2_ref, sh2_ref, att_ref, wuv_ref, u_ref, v_ref, ws_ref, bs_ref, wout_ref, lng_ref, lnb_ref,
         x1_out, h2_out, cm_scr, att_scr) = refs
    else:
        (x_ref, g1_ref, sc2_ref, sh2_ref, att_ref, u_ref, v_ref, ws_ref, bs_ref, wout_ref, lng_ref, lnb_ref,
         x1_out, h2_out, cm_scr) = refs
    bb, tt, d = x_ref.shape
    tm = bb * tt
    mb = bs_ref.shape[0]
    d_cmlp = u_ref.shape[1]
    d_head = d_cmlp // N_CMLP_HEADS

    if absorbed:
        for hd in range(N_HEADS):
            o_lat = att_ref[:, hd].reshape(tm, att_ref.shape[-1]).astype(BF16)
            att_scr[:, hd * D_V:(hd + 1) * D_V] = _dot(o_lat, wuv_ref[hd]).astype(BF16)
        mix = _dot(att_scr[...], wout_ref[0:d_att, :])
    else:
        mix = _dot(att_ref[...], wout_ref[0:d_att, :])

    r = lax.broadcasted_iota(jnp.int32, (mb, mb), 0)
    c = lax.broadcasted_iota(jnp.int32, (mb, mb), 1)
    shift = run.bit_length() - 1
    keep = ((r >> shift) == (c >> shift)) & ((c & (run - 1)) <= (r & (run - 1)))
    bs = bs_ref[...]
    for hd in range(N_CMLP_HEADS):
        w_rows = ws_ref[hd]
        w_blk = jnp.concatenate([w_rows] * (mb // w_rows.shape[0]), axis=0)
        w_blk = jnp.concatenate([w_blk] * (mb // w_rows.shape[1]), axis=1)
        wm = jnp.where(keep, w_blk, 0.0).astype(BF16)
        bias = bs[:, hd:hd + 1]
        cols = slice(hd * d_head, (hd + 1) * d_head)
        for blk in range(tm // mb):
            rows = slice(blk * mb, (blk + 1) * mb)
            mixed = _dot(wm, v_ref[rows, cols].astype(BF16)) + bias
            cm_scr[rows, cols] = (u_ref[rows, cols] * mixed).astype(BF16)

    mix = mix + _dot(cm_scr[...], wout_ref[d_att:d_att + d_cmlp, :])
    y = alpha * x_ref[...] + g1_ref[...] * mix.reshape(bb, tt, d)
    x1 = _layernorm(y, lng_ref[...], lnb_ref[...])
    x1_out[...] = x1
    h2 = x1 * (1.0 + sc2_ref[...]) + sh2_ref[...]
    h2_out[...] = h2.reshape(tm, d).astype(BF16)


def _mix(x, g1, sc2, sh2, att, u_act, v_ln, w, *, absorbed, alpha, tm_target):
    b, t, d = x.shape
    d_cmlp = u_act.shape[1]
    d_att = N_HEADS * D_V
    run = min(t, CHUNK)
    assert run & (run - 1) == 0 and t % run == 0
    if absorbed:
        tt, bb = t, _largest_tile(b, max(1, tm_target // t), 1)
    else:
        tt, bb = _largest_tile(t, tm_target, CHUNK), 1
    nt = t // tt
    tm = bb * tt
    mb = w["bs_rep"].shape[0]
    assert tm % mb == 0
    n_tok = b * t
    row3 = lambda i: (i // nt, i % nt, 0)
    mod3 = lambda i: (i // nt, 0, 0)
    row2 = lambda i: (i, 0)
    in_specs = [
        pl.BlockSpec((bb, tt, d), row3),
        pl.BlockSpec((bb, 1, d), mod3),
        pl.BlockSpec((bb, 1, d), mod3),
        pl.BlockSpec((bb, 1, d), mod3),
    ]
    args = [x, g1, sc2, sh2]
    if absorbed:
        in_specs += [pl.BlockSpec((bb, N_HEADS, t, att.shape[-1]), lambda i: (i, 0, 0, 0)),
                     _const_spec(w["w_uv_h"].shape)]
        args += [att, w["w_uv_h"]]
    else:
        in_specs.append(pl.BlockSpec((tm, d_att), row2))
        args.append(att)
    in_specs += [
        pl.BlockSpec((tm, d_cmlp), row2),
        pl.BlockSpec((tm, d_cmlp), row2),
        _const_spec(w["ws_rep"].shape),
        _const_spec(w["bs_rep"].shape),
        _const_spec(w["w_out"].shape),
        _const_spec((1, d)),
        _const_spec((1, d)),
    ]
    args += [u_act, v_ln, w["ws_rep"], w["bs_rep"], w["w_out"], w["ln1_g"], w["ln1_b"]]
    kern = functools.partial(_mix_kernel, absorbed=absorbed, alpha=alpha, run=run, d_att=d_att)
    return pl.pallas_call(
        kern, grid=(n_tok // tm,), in_specs=in_specs,
        out_specs=[pl.BlockSpec((bb, tt, d), row3), pl.BlockSpec((tm, d), row2)],
        out_shape=[jax.ShapeDtypeStruct((b, t, d), F32), jax.ShapeDtypeStruct((n_tok, d), BF16)],
        scratch_shapes=[pltpu.VMEM((tm, d_cmlp), BF16)] + ([pltpu.VMEM((tm, d_att), BF16)] if absorbed else []),
        compiler_params=_params(("arbitrary",)),
        name="mix_sample" if absorbed else "mix_prompt",
    )(*args)


def _ffn_kernel(h_ref, x1_ref, g2_ref, wg_ref, wu_ref, wd_ref, lng_ref, lnb_ref, o_ref, *rest, alpha, emit_bf16):
    f = pl.program_id(1)
    bb, tt, d = x1_ref.shape

    @pl.when(f == 0)
    def _():
        o_ref[...] = jnp.zeros(o_ref.shape, F32)

    wg, wu, wd = wg_ref[...], wu_ref[...], wd_ref[...]
    if emit_bf16:
        wg, wu, wd = wg.astype(BF16), wu.astype(BF16), wd.astype(BF16)
        wg_out, wu_out, wd_out = rest
        wg_out[...] = wg
        wu_out[...] = wu
        wd_out[...] = wd
    h = h_ref[...]
    a = (jax.nn.silu(_dot(h, wg)) * _dot(h, wu)).astype(BF16)
    o_ref[...] += _dot(a, wd).reshape(bb, tt, d)

    @pl.when(f == pl.num_programs(1) - 1)
    def _():
        y = alpha * x1_ref[...] + g2_ref[...] * o_ref[...]
        o_ref[...] = _layernorm(y, lng_ref[...], lnb_ref[...])


def _ffn(h2, x1, g2, w, w_gate, w_up, w_down, *, absorbed, alpha, tm_target, tf_target):
    b, t, d = x1.shape
    d_ff = w_gate.shape[1]
    emit_bf16 = w_gate.dtype != BF16
    if absorbed:
        tt, bb = t, _largest_tile(b, max(1, tm_target // t), 1)
    else:
        tt, bb = _largest_tile(t, tm_target, 16), 1
    nt = t // tt
    tm = bb * tt
    n_row_tiles = b * t // tm
    tf = _largest_tile(d_ff, tf_target, LANES)
    row3 = lambda i, f: (i // nt, i % nt, 0)
    w_col = pl.BlockSpec((d, tf), lambda i, f: (0, f))
    w_row = pl.BlockSpec((tf, d), lambda i, f: (f, 0))
    once = {"pipeline_mode": pl.Buffered(1)} if n_row_tiles == 1 else {}
    out_specs = [pl.BlockSpec((bb, tt, d), row3)]
    out_shape = [jax.ShapeDtypeStruct((b, t, d), F32)]
    if emit_bf16:
        out_specs += [w_col, w_col, w_row]
        out_shape += [jax.ShapeDtypeStruct(a.shape, BF16) for a in (w_gate, w_up, w_down)]
    return pl.pallas_call(
        functools.partial(_ffn_kernel, alpha=alpha, emit_bf16=emit_bf16),
        grid=(n_row_tiles, d_ff // tf),
        in_specs=[
            pl.BlockSpec((tm, d), lambda i, f: (i, 0), **once),
            pl.BlockSpec((bb, tt, d), row3, **once),
            pl.BlockSpec((bb, 1, d), lambda i, f: (i // nt, 0, 0)),
            w_col, w_col, w_row,
            _const_spec((1, d)),
            _const_spec((1, d)),
        ],
        out_specs=out_specs,
        out_shape=out_shape,
        compiler_params=_params(("arbitrary", "arbitrary")),
        name="ffn_sample" if absorbed else "ffn_prompt",
    )(h2, x1, g2, w_gate, w_up, w_down, w["ln2_g"], w["ln2_b"])


def _rope_pair(a):
    half = D_ROPE // 2
    return jnp.concatenate([a, -a[..., half:], a[..., :half]], axis=-1)


def _layer_weights(l, w_in, q_norm, w_uq, kv_norm, w_uk, w_uv, v_ln_g, v_ln_b, w_s, b_s, w_out, ln1_g, ln1_b,
                   w_gate, w_up, w_down, ln2_g, ln2_b, t_prompt, t_sample, mb_sample):
    q_lora, kv_lora, d_cmlp = q_norm.shape[-1], kv_norm.shape[-1], v_ln_g.shape[-1]
    wi = w_in[l]
    i1 = q_lora + kv_lora
    i2 = i1 + D_ROPE
    wq = w_uq[l].reshape(q_lora, N_HEADS, D_NOPE + D_ROPE)
    w_q = jnp.concatenate([wq[..., :D_NOPE].reshape(q_lora, -1),
                           _rope_pair(wq[..., D_NOPE:]).reshape(q_lora, -1)], axis=1).astype(BF16)
    row = lambda a: a[l].reshape(1, -1)

    def mixing(t, mb):
        run = min(t, CHUNK)
        return (jnp.tile(w_s[l][:, :run, :run], (1, 1, max(1, min(LANES, mb) // run))),
                jnp.tile(b_s[l][:, :run], (1, mb // run)).T)

    ws_p, bs_p = mixing(t_prompt, min(t_prompt, CHUNK))
    ws_s, bs_s = mixing(t_sample, mb_sample)
    common = {
        "w_lat": wi[:, :i1].astype(BF16), "w_uv_in": wi[:, i2:].astype(BF16),
        "w_kr": _rope_pair(wi[:, i1:i2]).astype(BF16),
        "w_q": w_q, "q_norm": row(q_norm), "kv_norm": row(kv_norm),
        "v_ln_g": row(v_ln_g), "v_ln_b": row(v_ln_b), "w_out": w_out[l].astype(BF16),
        "ln1_g": row(ln1_g), "ln1_b": row(ln1_b), "ln2_g": row(ln2_g), "ln2_b": row(ln2_b),
    }
    prompt = dict(common, w_uk_f=w_uk[l].reshape(kv_lora, -1).astype(BF16),
                  w_uv_f=w_uv[l].reshape(kv_lora, -1).astype(BF16), ws_rep=ws_p, bs_rep=bs_p)
    sample = dict(common, w_uk_t=jnp.transpose(w_uk[l], (1, 2, 0)).astype(BF16),
                  w_uv_h=jnp.transpose(w_uv[l], (1, 0, 2)).astype(BF16), ws_rep=ws_s, bs_rep=bs_s)
    return prompt, sample


def _rope_tables(pos):
    inv = ROPE_THETA ** (-jnp.arange(0, D_ROPE, 2, dtype=F32) / D_ROPE)
    ang = pos[:, None] * inv[None, :]
    z = jnp.zeros((pos.shape[0], D_ROPE), F32)
    cos, sin = jnp.cos(ang), jnp.sin(ang)
    return jnp.concatenate([cos, cos, z], axis=1), jnp.concatenate([sin, sin, z], axis=1)


def _mods(mod):
    return [m[:, None, :] for m in jnp.split(mod, 6, axis=-1)]


def kernel(x_prompt, x_sample, cache_latent, cache_krope, page_table, c_prompt, c_sample, w_ada, b_ada, w_in, q_norm,
           w_uq, kv_norm, w_uk, w_uv, v_ln_g, v_ln_b, w_s, b_s, w_out, ln1_g, ln1_b, w_gate, w_up, w_down, ln2_g,
           ln2_b):
    depth = w_in.shape[0]
    alpha = (2.0 * depth) ** 0.25
    bp, tp, d = x_prompt.shape
    bs, ts, _ = x_sample.shape
    past_len = page_table.shape[1] * PAGE_SIZE
    kv_lora = kv_norm.shape[-1]
    d_cmlp = v_ln_g.shape[-1]
    cdt = cache_latent.dtype

    xp = x_prompt.astype(F32)
    xs = x_sample.astype(F32)
    cos_p, sin_p = _rope_tables(jnp.arange(tp, dtype=F32))
    cos_s, sin_s = _rope_tables(past_len + jnp.arange(ts, dtype=F32))
    c_all = jnp.concatenate([c_prompt, c_sample], axis=0).astype(F32)
    krope_t = jnp.swapaxes(cache_krope, 2, 3)
    tm_s = _largest_tile(bs, max(1, 256 // ts), 1) * ts

    lat_p, kr_p, v_p, lat_s, kr_s, v_s = [], [], [], [], [], []
    for l in range(depth):
        wp, ws = _layer_weights(l, w_in, q_norm, w_uq, kv_norm, w_uk, w_uv, v_ln_g, v_ln_b, w_s, b_s, w_out, ln1_g,
                                ln1_b, w_gate, w_up, w_down, ln2_g, ln2_b, tp, ts, tm_s)
        mod = _ada(c_all, w_ada[l], b_ada[l])
        sh1, sc1, g1, sh2, sc2, g2 = _mods(mod[:bp])
        ckv, kr, u_act, v_ln, q_cat, k_cat, v_h = _inproj(xp, sc1, sh1, cos_p, sin_p, wp, absorbed=False,
                                                          tm_target=256)
        att = _flash(q_cat, k_cat, v_h, bp, tp)
        x1_p, h2_p = _mix(xp, g1, sc2, sh2, att, u_act, v_ln, wp, absorbed=False, alpha=alpha, tm_target=512)
        g2_p = g2
        lat_p.append(ckv.reshape(bp, tp, kv_lora))
        kr_p.append(kr.reshape(bp, tp, D_ROPE))
        v_p.append(v_ln.reshape(bp, tp, d_cmlp)[:, tp - min(tp, CHUNK):])

        sh1, sc1, g1, sh2, sc2, g2 = _mods(mod[bp:])
        ckv, kr, u_act, v_ln, q_cat = _inproj(xs, sc1, sh1, cos_s, sin_s, ws, absorbed=True, tm_target=tm_s)
        o_lat = _paged(q_cat, ckv, kr, cache_latent, krope_t, page_table, l)
        x1, h2 = _mix(xs, g1, sc2, sh2, o_lat, u_act, v_ln, ws, absorbed=True, alpha=alpha, tm_target=tm_s)
        xs, wg_bf, wu_bf, wd_bf = _ffn(h2, x1, g2, ws, w_gate[l], w_up[l], w_down[l], absorbed=True, alpha=alpha,
                                       tm_target=1024, tf_target=256)
        xp = _ffn(h2_p, x1_p, g2_p, wp, wg_bf, wu_bf, wd_bf, absorbed=False, alpha=alpha, tm_target=512,
                  tf_target=512)[0]
        lat_s.append(ckv.reshape(bs, ts, kv_lora))
        kr_s.append(kr.reshape(bs, ts, D_ROPE))
        v_s.append(v_ln.reshape(bs, ts, d_cmlp))

    return (xp.astype(x_prompt.dtype), xs.astype(x_sample.dtype),
            jnp.stack(lat_p).astype(cdt), jnp.stack(kr_p).astype(cdt), jnp.stack(v_p).astype(cdt),
            jnp.stack(lat_s).astype(cdt), jnp.stack(kr_s).astype(cdt), jnp.stack(v_s).astype(cdt))
```

```python
import functools
import math

import jax
import jax.numpy as jnp
from jax import lax
from jax.experimental import pallas as pl
from jax.experimental.pallas import tpu as pltpu

N_HEADS = 8
D_NOPE = 128
D_ROPE = 64
D_V = 128
N_CMLP_HEADS = 8
CHUNK = 128
PAGE_SIZE = 128
ROPE_THETA = 10000.0
RMS_EPS = 1e-6
LN_EPS = 1e-5
NEG = -1e30
SM_SCALE = 1.0 / math.sqrt(D_NOPE + D_ROPE)

LANES = 128
SUBLANES = 8
D_QK = 2 * LANES
VMEM_LIMIT = 56 * 1024 * 1024

F32 = jnp.float32
BF16 = jnp.bfloat16


def _largest_tile(n, target, multiple):
    if n <= target:
        return n
    t = (target // multiple) * multiple
    while t >= multiple:
        if n % t == 0:
            return t
        t -= multiple
    raise ValueError(f"no tile for {n}")


def _const_spec(shape):
    nd = len(shape)
    return pl.BlockSpec(shape, lambda *_: (0,) * nd, pipeline_mode=pl.Buffered(1))


def _params(semantics):
    return pltpu.CompilerParams(dimension_semantics=semantics, vmem_limit_bytes=VMEM_LIMIT)


def _layernorm(x, g, b):
    mu = jnp.mean(x, axis=-1, keepdims=True)
    xc = x - mu
    var = jnp.mean(xc * xc, axis=-1, keepdims=True)
    return xc * lax.rsqrt(var + LN_EPS) * g + b


def _rmsnorm(x, g):
    return x * lax.rsqrt(jnp.mean(x * x, axis=-1, keepdims=True) + RMS_EPS) * g


def _dot(a, b):
    return jnp.dot(a, b, preferred_element_type=F32)


def _dot_nt(a, b):
    return lax.dot_general(a, b, (((1,), (1,)), ((), ())), preferred_element_type=F32)


def _ada_kernel(c_ref, w_ref, b_ref, o_ref):
    c = c_ref[...]
    o_ref[...] = _dot(jax.nn.silu(c).astype(BF16), w_ref[...].astype(BF16)) + b_ref[...]


def _ada(c_all, w_ada, b_ada):
    n, d = c_all.shape
    dout = w_ada.shape[1]
    tn = _largest_tile(dout, 1536, LANES)
    return pl.pallas_call(
        _ada_kernel,
        grid=(dout // tn,),
        in_specs=[
            _const_spec((n, d)),
            pl.BlockSpec((d, tn), lambda j: (0, j)),
            pl.BlockSpec((1, tn), lambda j: (0, j)),
        ],
        out_specs=pl.BlockSpec((n, tn), lambda j: (0, j)),
        out_shape=jax.ShapeDtypeStruct((n, dout), F32),
        compiler_params=_params(("arbitrary",)),
        name="ada",
    )(c_all, w_ada, b_ada.reshape(1, dout))


def _rope_tile(t, cos, sin, bb, tt):
    rolled = pltpu.roll(t, D_ROPE, axis=1)
    r = t.reshape(bb, tt, LANES) * cos[None] + rolled.reshape(bb, tt, LANES) * sin[None]
    return r.reshape(bb * tt, LANES)


def _inproj_kernel(*refs, absorbed, q_lora, kv_lora, d_cmlp):
    if absorbed:
        (x_ref, sc_ref, sh_ref, cos_ref, sin_ref, wlat_ref, wuv_ref, wkr_ref, qn_ref, kvn_ref, vg_ref, vb_ref,
         wq_ref, wk_ref, ckv_out, kr_out, u_out, v_out, q_out) = refs
    else:
        (x_ref, sc_ref, sh_ref, cos_ref, sin_ref, wlat_ref, wuv_ref, wkr_ref, qn_ref, kvn_ref, vg_ref, vb_ref,
         wq_ref, wk_ref, wv_ref, ckv_out, kr_out, u_out, v_out, q_out, k_out, vh_out) = refs
    bb, tt, d = x_ref.shape
    tm = bb * tt
    cos = cos_ref[...]
    sin = sin_ref[...]

    h = x_ref[...] * (1.0 + sc_ref[...]) + sh_ref[...]
    h = h.reshape(tm, d).astype(BF16)

    ckv = _rmsnorm(_dot(h, wlat_ref[:, q_lora:q_lora + kv_lora]), kvn_ref[...])
    ckv_out[...] = ckv
    ckv_bf = ckv.astype(BF16)

    u_out[...] = jax.nn.gelu(_dot(h, wuv_ref[:, 0:d_cmlp]))
    v_out[...] = _layernorm(jax.nn.gelu(_dot(h, wuv_ref[:, d_cmlp:2 * d_cmlp])), vg_ref[...], vb_ref[...])

    kro = _rope_tile(_dot(h, wkr_ref[...]), cos, sin, bb, tt)
    kr_out[...] = kro[:, :D_ROPE]

    cq = _rmsnorm(_dot(h, wlat_ref[:, 0:q_lora]), qn_ref[...]).astype(BF16)
    n_nope = N_HEADS * D_NOPE
    q_nope = _dot(cq, wq_ref[:, 0:n_nope])
    q_rt = _dot(cq, wq_ref[:, n_nope:n_nope + N_HEADS * LANES])

    if absorbed:
        for hd in range(N_HEADS):
            qn_h = q_nope[:, hd * D_NOPE:(hd + 1) * D_NOPE].astype(BF16)
            q_lat = _dot(qn_h, wk_ref[hd]) * SM_SCALE
            r_h = _rope_tile(q_rt[:, hd * LANES:(hd + 1) * LANES], cos, sin, bb, tt) * SM_SCALE
            q_out[:, hd, :, 0:kv_lora] = q_lat.reshape(bb, tt, kv_lora)
            q_out[:, hd, :, kv_lora:kv_lora + LANES] = r_h.reshape(bb, tt, LANES)
    else:
        k_nope = _dot(ckv_bf, wk_ref[...])
        vh_out[...] = _dot(ckv_bf, wv_ref[...]).astype(BF16)
        kro_bf = kro.astype(BF16)
        for hd in range(N_HEADS):
            r_h = _rope_tile(q_rt[:, hd * LANES:(hd + 1) * LANES], cos, sin, bb, tt) * SM_SCALE
            q_out[:, hd * D_QK:hd * D_QK + D_NOPE] = (q_nope[:, hd * D_NOPE:(hd + 1) * D_NOPE] * SM_SCALE).astype(BF16)
            q_out[:, hd * D_QK + D_NOPE:(hd + 1) * D_QK] = r_h.astype(BF16)
            k_out[:, hd * D_QK:hd * D_QK + D_NOPE] = k_nope[:, hd * D_NOPE:(hd + 1) * D_NOPE].astype(BF16)
            k_out[:, hd * D_QK + D_NOPE:(hd + 1) * D_QK] = kro_bf


def _inproj(x, sc, sh, cos_t, sin_t, w, *, absorbed, tm_target):
    b, t, d = x.shape
    q_lora, kv_lora, d_cmlp = w["q_norm"].shape[-1], w["kv_norm"].shape[-1], w["v_ln_g"].shape[-1]
    if absorbed:
        tt, bb = t, _largest_tile(b, max(1, tm_target // t), 1)
    else:
        tt, bb = _largest_tile(t, tm_target, 16), 1
    nt = t // tt
    tm = bb * tt
    n_tok = b * t
    grid = (n_tok // tm,)

    row3 = lambda i: (i // nt, i % nt, 0)
    mod3 = lambda i: (i // nt, 0, 0)
    row2 = lambda i: (i, 0)
    in_specs = [
        pl.BlockSpec((bb, tt, d), row3),
        pl.BlockSpec((bb, 1, d), mod3),
        pl.BlockSpec((bb, 1, d), mod3),
        pl.BlockSpec((tt, LANES), lambda i: (i % nt, 0)),
        pl.BlockSpec((tt, LANES), lambda i: (i % nt, 0)),
        _const_spec(w["w_lat"].shape),
        _const_spec(w["w_uv_in"].shape),
        _const_spec(w["w_kr"].shape),
        _const_spec((1, q_lora)),
        _const_spec((1, kv_lora)),
        _const_spec((1, d_cmlp)),
        _const_spec((1, d_cmlp)),
        _const_spec(w["w_q"].shape),
    ]
    args = [x, sc, sh, cos_t, sin_t, w["w_lat"], w["w_uv_in"], w["w_kr"], w["q_norm"], w["kv_norm"], w["v_ln_g"],
            w["v_ln_b"], w["w_q"]]
    out_specs = [
        pl.BlockSpec((tm, kv_lora), row2),
        pl.BlockSpec((tm, D_ROPE), row2),
        pl.BlockSpec((tm, d_cmlp), row2),
        pl.BlockSpec((tm, d_cmlp), row2),
    ]
    out_shape = [
        jax.ShapeDtypeStruct((n_tok, kv_lora), F32),
        jax.ShapeDtypeStruct((n_tok, D_ROPE), F32),
        jax.ShapeDtypeStruct((n_tok, d_cmlp), F32),
        jax.ShapeDtypeStruct((n_tok, d_cmlp), F32),
    ]
    if absorbed:
        in_specs.append(_const_spec(w["w_uk_t"].shape))
        args.append(w["w_uk_t"])
        d_q = kv_lora + LANES
        out_specs.append(pl.BlockSpec((bb, N_HEADS, t, d_q), lambda i: (i, 0, 0, 0)))
        out_shape.append(jax.ShapeDtypeStruct((b, N_HEADS, t, d_q), F32))
    else:
        in_specs += [_const_spec(w["w_uk_f"].shape), _const_spec(w["w_uv_f"].shape)]
        args += [w["w_uk_f"], w["w_uv_f"]]
        out_specs += [
            pl.BlockSpec((tm, N_HEADS * D_QK), row2),
            pl.BlockSpec((tm, N_HEADS * D_QK), row2),
            pl.BlockSpec((tm, N_HEADS * D_V), row2),
        ]
        out_shape += [
            jax.ShapeDtypeStruct((n_tok, N_HEADS * D_QK), BF16),
            jax.ShapeDtypeStruct((n_tok, N_HEADS * D_QK), BF16),
            jax.ShapeDtypeStruct((n_tok, N_HEADS * D_V), BF16),
        ]
    kern = functools.partial(_inproj_kernel, absorbed=absorbed, q_lora=q_lora, kv_lora=kv_lora, d_cmlp=d_cmlp)
    return pl.pallas_call(
        kern, grid=grid, in_specs=in_specs, out_specs=out_specs, out_shape=out_shape,
        compiler_params=_params(("arbitrary",)),
        name="inproj_sample" if absorbed else "inproj_prompt",
    )(*args)


def _flash_kernel(q_ref, k_ref, v_ref, o_ref):
    tq = q_ref.shape[0]
    t = k_ref.shape[0]
    qi = pl.program_id(2)
    row = lax.broadcasted_iota(jnp.int32, (tq, tq), 0)
    col = lax.broadcasted_iota(jnp.int32, (tq, tq), 1)

    def one_head(hh, n_past):
        qk = slice(hh * D_QK, (hh + 1) * D_QK)
        vo = slice(hh * D_V, (hh + 1) * D_V)
        diag = pl.ds(n_past, tq)
        q = q_ref[:, qk]
        s_d = jnp.where(col <= row, _dot_nt(q, k_ref[diag, qk]), NEG)
        m = jnp.max(s_d, axis=-1, keepdims=True)
        if n_past:
            s_p = _dot_nt(q, k_ref[0:n_past, qk])
            m = jnp.maximum(m, jnp.max(s_p, axis=-1, keepdims=True))
            p_p = jnp.exp(s_p - m)
        p_d = jnp.exp(s_d - m)
        l = jnp.sum(p_d, axis=-1, keepdims=True)
        o = _dot(p_d.astype(BF16), v_ref[diag, vo])
        if n_past:
            l = l + jnp.sum(p_p, axis=-1, keepdims=True)
            o = o + _dot(p_p.astype(BF16), v_ref[0:n_past, vo])
        o_ref[:, vo] = (o / l).astype(o_ref.dtype)

    for c in range(t // tq):
        @pl.when(qi == c)
        def _(n_past=c * tq):
            for hh in range(FLASH_HEADS):
                one_head(hh, n_past)


FLASH_HEADS = 4


def _flash(q_cat, k_cat, v_h, b, t):
    tq = _largest_tile(t, 512, 16)
    nq = t // tq
    hp = FLASH_HEADS
    return pl.pallas_call(
        _flash_kernel,
        grid=(b, N_HEADS // hp, nq),
        in_specs=[
            pl.BlockSpec((tq, hp * D_QK), lambda bi, h, i: (bi * nq + i, h)),
            pl.BlockSpec((t, hp * D_QK), lambda bi, h, i: (bi, h)),
            pl.BlockSpec((t, hp * D_V), lambda bi, h, i: (bi, h)),
        ],
        out_specs=pl.BlockSpec((tq, hp * D_V), lambda bi, h, i: (bi * nq + i, h)),
        out_shape=jax.ShapeDtypeStruct((b * t, N_HEADS * D_V), BF16),
        compiler_params=_params(("arbitrary", "arbitrary", "arbitrary")),
        name="flash_prompt",
    )(q_cat, k_cat, v_h)


PAGE_RING = 4


def _paged_kernel(pt_ref, q_ref, ckv_ref, kr_ref, lat_hbm, krp_hbm, o_ref, lat_buf, krp_buf, sem,
                  *, layer, n_group, n_steps, kv_lora):
    b = pl.program_id(0)
    nb = pl.num_programs(0)
    s_new = ckv_ref.shape[1]
    n_rows = N_HEADS * s_new

    def page_copies(bi, g):
        slot = g % PAGE_RING
        copies = []
        for k in range(n_group):
            page = pt_ref[bi, g * n_group + k]
            rows = pl.ds(k * PAGE_SIZE, PAGE_SIZE)
            copies.append(pltpu.make_async_copy(lat_hbm.at[layer, page], lat_buf.at[slot, rows, :], sem.at[0, slot]))
            copies.append(pltpu.make_async_copy(krp_hbm.at[layer, page], krp_buf.at[slot, :, rows], sem.at[1, slot]))
        return copies

    def start_all(copies):
        for i, cp in enumerate(copies):
            cp.start(priority=(i // 2) % 2)

    def start_group(g):
        if g < n_steps:
            start_all(page_copies(b, g))
        else:
            @pl.when(b + 1 < nb)
            def _():
                start_all(page_copies(b + 1, g - n_steps))

    @pl.when(b == 0)
    def _():
        for g in range(PAGE_RING - 1):
            start_group(g)

    q = q_ref[0].reshape(n_rows, kv_lora + LANES)
    q_lat = q[:, :kv_lora].astype(BF16)
    q_rope = q[:, kv_lora:kv_lora + D_ROPE].astype(BF16)

    def scores(g):
        slot = g % PAGE_RING
        return _dot_nt(q_lat, lat_buf[slot].astype(BF16)) + _dot(q_rope, krp_buf[slot].astype(BF16))

    def attend(state, s, kv_bf):
        m_old, l_old, acc = state
        m_new = jnp.maximum(m_old, jnp.max(s, axis=-1, keepdims=True))
        corr = jnp.exp(m_old - m_new)
        p = jnp.exp(s - m_new)
        l_new = l_old * corr + jnp.sum(p, axis=-1, keepdims=True)
        return m_new, l_new, acc * corr + _dot(p.astype(BF16), kv_bf)

    pad = PAGE_SIZE - s_new
    kv0 = jnp.concatenate([ckv_ref[0], jnp.zeros((pad, kv_lora), F32)], axis=0).astype(BF16)
    kr0 = jnp.concatenate([kr_ref[0], jnp.zeros((pad, D_ROPE), F32)], axis=0).astype(BF16)
    s0 = _dot_nt(q_lat, kv0) + _dot_nt(q_rope, kr0)
    qpos = lax.broadcasted_iota(jnp.int32, (n_rows, PAGE_SIZE), 0) % s_new
    kpos = lax.broadcasted_iota(jnp.int32, (n_rows, PAGE_SIZE), 1)
    s0 = jnp.where(kpos <= qpos, s0, NEG)
    state = (jnp.full((n_rows, 1), NEG, F32), jnp.zeros((n_rows, 1), F32), jnp.zeros((n_rows, kv_lora), F32))
    state = attend(state, s0, kv0)

    for cp in page_copies(b, 0):
        cp.wait()
    s = scores(0)
    for g in range(n_steps):
        start_group(g + PAGE_RING - 1)
        if g + 1 < n_steps:
            for cp in page_copies(b, g + 1):
                cp.wait()
            s_next = scores(g + 1)
        state = attend(state, s, lat_buf[g % PAGE_RING].astype(BF16))
        if g + 1 < n_steps:
            s = s_next

    _, l_fin, acc = state
    o_ref[0] = (acc / l_fin).reshape(N_HEADS, s_new, kv_lora)


def _paged(q_cat, ckv_new, kr_new, cache_latent, cache_krope_t, page_table, layer):
    b, _, s_new, d_q = q_cat.shape
    kv_lora = ckv_new.shape[-1]
    n_pages = page_table.shape[1]
    assert n_pages % PAGE_RING == 0
    n_group = _largest_tile(n_pages // PAGE_RING, 16, 1)
    n_steps = n_pages // n_group
    grid_spec = pltpu.PrefetchScalarGridSpec(
        num_scalar_prefetch=1,
        grid=(b,),
        in_specs=[
            pl.BlockSpec((1, N_HEADS, s_new, d_q), lambda bi, pt: (bi, 0, 0, 0)),
            pl.BlockSpec((1, s_new, kv_lora), lambda bi, pt: (bi, 0, 0)),
            pl.BlockSpec((1, s_new, D_ROPE), lambda bi, pt: (bi, 0, 0)),
            pl.BlockSpec(memory_space=pl.ANY),
            pl.BlockSpec(memory_space=pl.ANY),
        ],
        out_specs=pl.BlockSpec((1, N_HEADS, s_new, kv_lora), lambda bi, pt: (bi, 0, 0, 0)),
        scratch_shapes=[
            pltpu.VMEM((PAGE_RING, n_group * PAGE_SIZE, kv_lora), F32),
            pltpu.VMEM((PAGE_RING, D_ROPE, n_group * PAGE_SIZE), F32),
            pltpu.SemaphoreType.DMA((2, PAGE_RING)),
        ],
    )
    kern = functools.partial(_paged_kernel, layer=layer, n_group=n_group, n_steps=n_steps, kv_lora=kv_lora)
    return pl.pallas_call(
        kern, grid_spec=grid_spec,
        out_shape=jax.ShapeDtypeStruct((b, N_HEADS, s_new, kv_lora), F32),
        compiler_params=_params(("arbitrary",)),
        name="paged_sample",
    )(page_table, q_cat, ckv_new.reshape(b, s_new, kv_lora), kr_new.reshape(b, s_new, D_ROPE),
      cache_latent, cache_krope_t)


def _mix_kernel(*refs, absorbed, alpha, run, d_att):
    if absorbed:
        (x_ref, g1_ref, sc2_ref, sh2_ref, att_ref, wuv_ref, u_ref, v_ref, ws_ref, bs_ref, wout_ref, lng_ref, lnb_ref,
         x1_out, h2_out, cm_scr, att_scr) = refs
    else:
        (x_ref, g1_ref, sc2_ref, sh2_ref, att_ref, u_ref, v_ref, ws_ref, bs_ref, wout_ref, lng_ref, lnb_ref,
         x1_out, h2_out, cm_scr) = refs
    bb, tt, d = x_ref.shape
    tm = bb * tt
    mb = bs_ref.shape[0]
    d_cmlp = u_ref.shape[1]
    d_head = d_cmlp // N_CMLP_HEADS

    if absorbed:
        for hd in range(N_HEADS):
            o_lat = att_ref[:, hd].reshape(tm, att_ref.shape[-1]).astype(BF16)
            att_scr[:, hd * D_V:(hd + 1) * D_V] = _dot(o_lat, wuv_ref[hd]).astype(BF16)
        mix = _dot(att_scr[...], wout_ref[0:d_att, :])
    else:
        mix = _dot(att_ref[...], wout_ref[0:d_att, :])

    r = lax.broadcasted_iota(jnp.int32, (mb, mb), 0)
    c = lax.broadcasted_iota(jnp.int32, (mb, mb), 1)
    shift = run.bit_length() - 1
    keep = ((r >> shift) == (c >> shift)) & ((c & (run - 1)) <= (r & (run - 1)))
    bs = bs_ref[...]
    for hd in range(N_CMLP_HEADS):
        w_rows = ws_ref[hd]
        w_blk = jnp.concatenate([w_rows] * (mb // w_rows.shape[0]), axis=0)
        w_blk = jnp.concatenate([w_blk] * (mb // w_rows.shape[1]), axis=1)
        wm = jnp.where(keep, w_blk, 0.0).astype(BF16)
        bias = bs[:, hd:hd + 1]
        cols = slice(hd * d_head, (hd + 1) * d_head)
        for blk in range(tm // mb):
            rows = slice(blk * mb, (blk + 1) * mb)
            mixed = _dot(wm, v_ref[rows, cols].astype(BF16)) + bias
            cm_scr[rows, cols] = (u_ref[rows, cols] * mixed).astype(BF16)

    mix = mix + _dot(cm_scr[...], wout_ref[d_att:d_att + d_cmlp, :])
    y = alpha * x_ref[...] + g1_ref[...] * mix.reshape(bb, tt, d)
    x1 = _layernorm(y, lng_ref[...], lnb_ref[...])
    x1_out[...] = x1
    h2 = x1 * (1.0 + sc2_ref[...]) + sh2_ref[...]
    h2_out[...] = h2.reshape(tm, d).astype(BF16)


def _mix(x, g1, sc2, sh2, att, u_act, v_ln, w, *, absorbed, alpha, tm_target):
    b, t, d = x.shape
    d_cmlp = u_act.shape[1]
    d_att = N_HEADS * D_V
    run = min(t, CHUNK)
    assert run & (run - 1) == 0 and t % run == 0
    if absorbed:
        tt, bb = t, _largest_tile(b, max(1, tm_target // t), 1)
    else:
        tt, bb = _largest_tile(t, tm_target, CHUNK), 1
    nt = t // tt
    tm = bb * tt
    mb = w["bs_rep"].shape[0]
    assert tm % mb == 0
    n_tok = b * t
    row3 = lambda i: (i // nt, i % nt, 0)
    mod3 = lambda i: (i // nt, 0, 0)
    row2 = lambda i: (i, 0)
    in_specs = [
        pl.BlockSpec((bb, tt, d), row3),
        pl.BlockSpec((bb, 1, d), mod3),
        pl.BlockSpec((bb, 1, d), mod3),
        pl.BlockSpec((bb, 1, d), mod3),
    ]
    args = [x, g1, sc2, sh2]
    if absorbed:
        in_specs += [pl.BlockSpec((bb, N_HEADS, t, att.shape[-1]), lambda i: (i, 0, 0, 0)),
                     _const_spec(w["w_uv_h"].shape)]
        args += [att, w["w_uv_h"]]
    else:
        in_specs.append(pl.BlockSpec((tm, d_att), row2))
        args.append(att)
    in_specs += [
        pl.BlockSpec((tm, d_cmlp), row2),
        pl.BlockSpec((tm, d_cmlp), row2),
        _const_spec(w["ws_rep"].shape),
        _const_spec(w["bs_rep"].shape),
        _const_spec(w["w_out"].shape),
        _const_spec((1, d)),
        _const_spec((1, d)),
    ]
    args += [u_act, v_ln, w["ws_rep"], w["bs_rep"], w["w_out"], w["ln1_g"], w["ln1_b"]]
    kern = functools.partial(_mix_kernel, absorbed=absorbed, alpha=alpha, run=run, d_att=d_att)
    return pl.pallas_call(
        kern, grid=(n_tok // tm,), in_specs=in_specs,
        out_specs=[pl.BlockSpec((bb, tt, d), row3), pl.BlockSpec((tm, d), row2)],
        out_shape=[jax.ShapeDtypeStruct((b, t, d), F32), jax.ShapeDtypeStruct((n_tok, d), BF16)],
        scratch_shapes=[pltpu.VMEM((tm, d_cmlp), BF16)] + ([pltpu.VMEM((tm, d_att), BF16)] if absorbed else []),
        compiler_params=_params(("arbitrary",)),
        name="mix_sample" if absorbed else "mix_prompt",
    )(*args)


def _ffn_kernel(h_ref, x1_ref, g2_ref, wg_ref, wu_ref, wd_ref, lng_ref, lnb_ref, o_ref, *rest, alpha, emit_bf16):
    f = pl.program_id(1)
    bb, tt, d = x1_ref.shape

    @pl.when(f == 0)
    def _():
        o_ref[...] = jnp.zeros(o_ref.shape, F32)

    wg, wu, wd = wg_ref[...], wu_ref[...], wd_ref[...]
    if emit_bf16:
        wg, wu, wd = wg.astype(BF16), wu.astype(BF16), wd.astype(BF16)
        wg_out, wu_out, wd_out = rest
        wg_out[...] = wg
        wu_out[...] = wu
        wd_out[...] = wd
    h = h_ref[...]
    a = (jax.nn.silu(_dot(h, wg)) * _dot(h, wu)).astype(BF16)
    o_ref[...] += _dot(a, wd).reshape(bb, tt, d)

    @pl.when(f == pl.num_programs(1) - 1)
    def _():
        y = alpha * x1_ref[...] + g2_ref[...] * o_ref[...]
        o_ref[...] = _layernorm(y, lng_ref[...], lnb_ref[...])


def _ffn(h2, x1, g2, w, w_gate, w_up, w_down, *, absorbed, alpha, tm_target, tf_target):
    b, t, d = x1.shape
    d_ff = w_gate.shape[1]
    emit_bf16 = w_gate.dtype != BF16
    if absorbed:
        tt, bb = t, _largest_tile(b, max(1, tm_target // t), 1)
    else:
        tt, bb = _largest_tile(t, tm_target, 16), 1
    nt = t // tt
    tm = bb * tt
    n_row_tiles = b * t // tm
    tf = _largest_tile(d_ff, tf_target, LANES)
    row3 = lambda i, f: (i // nt, i % nt, 0)
    w_col = pl.BlockSpec((d, tf), lambda i, f: (0, f))
    w_row = pl.BlockSpec((tf, d), lambda i, f: (f, 0))
    once = {"pipeline_mode": pl.Buffered(1)} if n_row_tiles == 1 else {}
    out_specs = [pl.BlockSpec((bb, tt, d), row3)]
    out_shape = [jax.ShapeDtypeStruct((b, t, d), F32)]
    if emit_bf16:
        out_specs += [w_col, w_col, w_row]
        out_shape += [jax.ShapeDtypeStruct(a.shape, BF16) for a in (w_gate, w_up, w_down)]
    return pl.pallas_call(
        functools.partial(_ffn_kernel, alpha=alpha, emit_bf16=emit_bf16),
        grid=(n_row_tiles, d_ff // tf),
        in_specs=[
            pl.BlockSpec((tm, d), lambda i, f: (i, 0), **once),
            pl.BlockSpec((bb, tt, d), row3, **once),
            pl.BlockSpec((bb, 1, d), lambda i, f: (i // nt, 0, 0)),
            w_col, w_col, w_row,
            _const_spec((1, d)),
            _const_spec((1, d)),
        ],
        out_specs=out_specs,
        out_shape=out_shape,
        compiler_params=_params(("arbitrary", "arbitrary")),
        name="ffn_sample" if absorbed else "ffn_prompt",
    )(h2, x1, g2, w_gate, w_up, w_down, w["ln2_g"], w["ln2_b"])


def _rope_pair(a):
    half = D_ROPE // 2
    return jnp.concatenate([a, -a[..., half:], a[..., :half]], axis=-1)


def _layer_weights(l, w_in, q_norm, w_uq, kv_norm, w_uk, w_uv, v_ln_g, v_ln_b, w_s, b_s, w_out, ln1_g, ln1_b,
                   w_gate, w_up, w_down, ln2_g, ln2_b, t_prompt, t_sample, mb_sample):
    q_lora, kv_lora, d_cmlp = q_norm.shape[-1], kv_norm.shape[-1], v_ln_g.shape[-1]
    wi = w_in[l]
    i1 = q_lora + kv_lora
    i2 = i1 + D_ROPE
    wq = w_uq[l].reshape(q_lora, N_HEADS, D_NOPE + D_ROPE)
    w_q = jnp.concatenate([wq[..., :D_NOPE].reshape(q_lora, -1),
                           _rope_pair(wq[..., D_NOPE:]).reshape(q_lora, -1)], axis=1).astype(BF16)
    row = lambda a: a[l].reshape(1, -1)

    def mixing(t, mb):
        run = min(t, CHUNK)
        return (jnp.tile(w_s[l][:, :run, :run], (1, 1, max(1, min(LANES, mb) // run))),
                jnp.tile(b_s[l][:, :run], (1, mb // run)).T)

    ws_p, bs_p = mixing(t_prompt, min(t_prompt, CHUNK))
    ws_s, bs_s = mixing(t_sample, mb_sample)
    common = {
        "w_lat": wi[:, :i1].astype(BF16), "w_uv_in": wi[:, i2:].astype(BF16),
        "w_kr": _rope_pair(wi[:, i1:i2]).astype(BF16),
        "w_q": w_q, "q_norm": row(q_norm), "kv_norm": row(kv_norm),
        "v_ln_g": row(v_ln_g), "v_ln_b": row(v_ln_b), "w_out": w_out[l].astype(BF16),
        "ln1_g": row(ln1_g), "ln1_b": row(ln1_b), "ln2_g": row(ln2_g), "ln2_b": row(ln2_b),
    }
    prompt = dict(common, w_uk_f=w_uk[l].reshape(kv_lora, -1).astype(BF16),
                  w_uv_f=w_uv[l].reshape(kv_lora, -1).astype(BF16), ws_rep=ws_p, bs_rep=bs_p)
    sample = dict(common, w_uk_t=jnp.transpose(w_uk[l], (1, 2, 0)).astype(BF16),
                  w_uv_h=jnp.transpose(w_uv[l], (1, 0, 2)).astype(BF16), ws_rep=ws_s, bs_rep=bs_s)
    return prompt, sample


def _rope_tables(pos):
    inv = ROPE_THETA ** (-jnp.arange(0, D_ROPE, 2, dtype=F32) / D_ROPE)
    ang = pos[:, None] * inv[None, :]
    z = jnp.zeros((pos.shape[0], D_ROPE), F32)
    cos, sin = jnp.cos(ang), jnp.sin(ang)
    return jnp.concatenate([cos, cos, z], axis=1), jnp.concatenate([sin, sin, z], axis=1)


def _mods(mod):
    return [m[:, None, :] for m in jnp.split(mod, 6, axis=-1)]


def kernel(x_prompt, x_sample, cache_latent, cache_krope, page_table, c_prompt, c_sample, w_ada, b_ada, w_in, q_norm,
           w_uq, kv_norm, w_uk, w_uv, v_ln_g, v_ln_b, w_s, b_s, w_out, ln1_g, ln1_b, w_gate, w_up, w_down, ln2_g,
           ln2_b):
    depth = w_in.shape[0]
    alpha = (2.0 * depth) ** 0.25
    bp, tp, d = x_prompt.shape
    bs, ts, _ = x_sample.shape
    past_len = page_table.shape[1] * PAGE_SIZE
    kv_lora = kv_norm.shape[-1]
    d_cmlp = v_ln_g.shape[-1]
    cdt = cache_latent.dtype

    xp = x_prompt.astype(F32)
    xs = x_sample.astype(F32)
    cos_p, sin_p = _rope_tables(jnp.arange(tp, dtype=F32))
    cos_s, sin_s = _rope_tables(past_len + jnp.arange(ts, dtype=F32))
    c_all = jnp.concatenate([c_prompt, c_sample], axis=0).astype(F32)
    krope_t = jnp.swapaxes(cache_krope, 2, 3)
    tm_s = _largest_tile(bs, max(1, 256 // ts), 1) * ts

    lat_p, kr_p, v_p, lat_s, kr_s, v_s = [], [], [], [], [], []
    for l in range(depth):
        wp, ws = _layer_weights(l, w_in, q_norm, w_uq, kv_norm, w_uk, w_uv, v_ln_g, v_ln_b, w_s, b_s, w_out, ln1_g,
                                ln1_b, w_gate, w_up, w_down, ln2_g, ln2_b, tp, ts, tm_s)
        mod = _ada(c_all, w_ada[l], b_ada[l])
        sh1, sc1, g1, sh2, sc2, g2 = _mods(mod[:bp])
        ckv, kr, u_act, v_ln, q_cat, k_cat, v_h = _inproj(xp, sc1, sh1, cos_p, sin_p, wp, absorbed=False,
                                                          tm_target=256)
        att = _flash(q_cat, k_cat, v_h, bp, tp)
        x1_p, h2_p = _mix(xp, g1, sc2, sh2, att, u_act, v_ln, wp, absorbed=False, alpha=alpha, tm_target=512)
        g2_p = g2
        lat_p.append(ckv.reshape(bp, tp, kv_lora))
        kr_p.append(kr.reshape(bp, tp, D_ROPE))
        v_p.append(v_ln.reshape(bp, tp, d_cmlp)[:, tp - min(tp, CHUNK):])

        sh1, sc1, g1, sh2, sc2, g2 = _mods(mod[bp:])
        ckv, kr, u_act, v_ln, q_cat = _inproj(xs, sc1, sh1, cos_s, sin_s, ws, absorbed=True, tm_target=tm_s)
        o_lat = _paged(q_cat, ckv, kr, cache_latent, krope_t, page_table, l)
        x1, h2 = _mix(xs, g1, sc2, sh2, o_lat, u_act, v_ln, ws, absorbed=True, alpha=alpha, tm_target=tm_s)
        xs, wg_bf, wu_bf, wd_bf = _ffn(h2, x1, g2, ws, w_gate[l], w_up[l], w_down[l], absorbed=True, alpha=alpha,
                                       tm_target=1024, tf_target=256)
        xp = _ffn(h2_p, x1_p, g2_p, wp, wg_bf, wu_bf, wd_bf, absorbed=False, alpha=alpha, tm_target=512,
                  tf_target=512)[0]
        lat_s.append(ckv.reshape(bs, ts, kv_lora))
        kr_s.append(kr.reshape(bs, ts, D_ROPE))
        v_s.append(v_ln.reshape(bs, ts, d_cmlp))

    return (xp.astype(x_prompt.dtype), xs.astype(x_sample.dtype),
            jnp.stack(lat_p).astype(cdt), jnp.stack(kr_p).astype(cdt), jnp.stack(v_p).astype(cdt),
            jnp.stack(lat_s).astype(cdt), jnp.stack(kr_s).astype(cdt), jnp.stack(v_s).astype(cdt))
```

```python
import functools
import math

import jax
import jax.numpy as jnp
from jax import lax
from jax.experimental import pallas as pl
from jax.experimental.pallas import tpu as pltpu

N_HEADS = 8
D_NOPE = 128
D_ROPE = 64
D_V = 128
N_CMLP_HEADS = 8
CHUNK = 128
PAGE_SIZE = 128
ROPE_THETA = 10000.0
RMS_EPS = 1e-6
LN_EPS = 1e-5
NEG = -1e30
SM_SCALE = 1.0 / math.sqrt(D_NOPE + D_ROPE)

LANES = 128
SUBLANES = 8
D_QK = 2 * LANES
VMEM_LIMIT = 56 * 1024 * 1024

F32 = jnp.float32
BF16 = jnp.bfloat16


def _largest_tile(n, target, multiple):
    if n <= target:
        return n
    t = (target // multiple) * multiple
    while t >= multiple:
        if n % t == 0:
            return t
        t -= multiple
    raise ValueError(f"no tile for {n}")


def _const_spec(shape):
    nd = len(shape)
    return pl.BlockSpec(shape, lambda *_: (0,) * nd, pipeline_mode=pl.Buffered(1))


def _params(semantics):
    return pltpu.CompilerParams(dimension_semantics=semantics, vmem_limit_bytes=VMEM_LIMIT)


def _layernorm(x, g, b):
    mu = jnp.mean(x, axis=-1, keepdims=True)
    xc = x - mu
    var = jnp.mean(xc * xc, axis=-1, keepdims=True)
    return xc * lax.rsqrt(var + LN_EPS) * g + b


def _rmsnorm(x, g):
    return x * lax.rsqrt(jnp.mean(x * x, axis=-1, keepdims=True) + RMS_EPS) * g


def _dot(a, b):
    return jnp.dot(a, b, preferred_element_type=F32)


def _dot_nt(a, b):
    return lax.dot_general(a, b, (((1,), (1,)), ((), ())), preferred_element_type=F32)


def _ada_kernel(c_ref, w_ref, b_ref, o_ref):
    c = c_ref[...]
    o_ref[...] = _dot(jax.nn.silu(c).astype(BF16), w_ref[...].astype(BF16)) + b_ref[...]


def _ada(c_all, w_ada, b_ada):
    n, d = c_all.shape
    dout = w_ada.shape[1]
    tn = _largest_tile(dout, 1536, LANES)
    return pl.pallas_call(
        _ada_kernel,
        grid=(dout // tn,),
        in_specs=[
            _const_spec((n, d)),
            pl.BlockSpec((d, tn), lambda j: (0, j)),
            pl.BlockSpec((1, tn), lambda j: (0, j)),
        ],
        out_specs=pl.BlockSpec((n, tn), lambda j: (0, j)),
        out_shape=jax.ShapeDtypeStruct((n, dout), F32),
        compiler_params=_params(("arbitrary",)),
        name="ada",
    )(c_all, w_ada, b_ada.reshape(1, dout))


def _rope_tile(t, cos, sin, bb, tt):
    rolled = pltpu.roll(t, D_ROPE, axis=1)
    r = t.reshape(bb, tt, LANES) * cos[None] + rolled.reshape(bb, tt, LANES) * sin[None]
    return r.reshape(bb * tt, LANES)


def _inproj_kernel(*refs, absorbed, q_lora, kv_lora, d_cmlp):
    if absorbed:
        (x_ref, sc_ref, sh_ref, cos_ref, sin_ref, wlat_ref, wuv_ref, wkr_ref, qn_ref, kvn_ref, vg_ref, vb_ref,
         wq_ref, wk_ref, ckv_out, kr_out, u_out, v_out, q_out) = refs
    else:
        (x_ref, sc_ref, sh_ref, cos_ref, sin_ref, wlat_ref, wuv_ref, wkr_ref, qn_ref, kvn_ref, vg_ref, vb_ref,
         wq_ref, wk_ref, wv_ref, ckv_out, kr_out, u_out, v_out, q_out, k_out, vh_out) = refs
    bb, tt, d = x_ref.shape
    tm = bb * tt
    cos = cos_ref[...]
    sin = sin_ref[...]

    h = x_ref[...] * (1.0 + sc_ref[...]) + sh_ref[...]
    h = h.reshape(tm, d).astype(BF16)

    ckv = _rmsnorm(_dot(h, wlat_ref[:, q_lora:q_lora + kv_lora]), kvn_ref[...])
    ckv_out[...] = ckv
    ckv_bf = ckv.astype(BF16)

    u_out[...] = jax.nn.gelu(_dot(h, wuv_ref[:, 0:d_cmlp]))
    v_out[...] = _layernorm(jax.nn.gelu(_dot(h, wuv_ref[:, d_cmlp:2 * d_cmlp])), vg_ref[...], vb_ref[...])

    kro = _rope_tile(_dot(h, wkr_ref[...]), cos, sin, bb, tt)
    kr_out[...] = kro

    cq = _rmsnorm(_dot(h, wlat_ref[:, 0:q_lora]), qn_ref[...]).astype(BF16)
    n_nope = N_HEADS * D_NOPE
    q_nope = _dot(cq, wq_ref[:, 0:n_nope])
    q_rt = _dot(cq, wq_ref[:, n_nope:n_nope + N_HEADS * LANES])

    if absorbed:
        for hd in range(N_HEADS):
            qn_h = q_nope[:, hd * D_NOPE:(hd + 1) * D_NOPE].astype(BF16)
            q_lat = _dot(qn_h, wk_ref[hd]) * SM_SCALE
            r_h = _rope_tile(q_rt[:, hd * LANES:(hd + 1) * LANES], cos, sin, bb, tt) * SM_SCALE
            q_out[:, hd, :, 0:kv_lora] = q_lat.reshape(bb, tt, kv_lora)
            q_out[:, hd, :, kv_lora:kv_lora + LANES] = r_h.reshape(bb, tt, LANES)
    else:
        k_nope = _dot(ckv_bf, wk_ref[...])
        vh_out[...] = _dot(ckv_bf, wv_ref[...]).astype(BF16)
        kro_bf = kro.astype(BF16)
        for hd in range(N_HEADS):
            r_h = _rope_tile(q_rt[:, hd * LANES:(hd + 1) * LANES], cos, sin, bb, tt) * SM_SCALE
            q_out[:, hd * D_QK:hd * D_QK + D_NOPE] = (q_nope[:, hd * D_NOPE:(hd + 1) * D_NOPE] * SM_SCALE).astype(BF16)
            q_out[:, hd * D_QK + D_NOPE:(hd + 1) * D_QK] = r_h.astype(BF16)
            k_out[:, hd * D_QK:hd * D_QK + D_NOPE] = k_nope[:, hd * D_NOPE:(hd + 1) * D_NOPE].astype(BF16)
            k_out[:, hd * D_QK + D_NOPE:(hd + 1) * D_QK] = kro_bf


def _inproj(x, sc, sh, cos_t, sin_t, w, *, absorbed, tm_target):
    b, t, d = x.shape
    q_lora, kv_lora, d_cmlp = w["q_norm"].shape[-1], w["kv_norm"].shape[-1], w["v_ln_g"].shape[-1]
    if absorbed:
        tt, bb = t, _largest_tile(b, max(1, tm_target // t), 1)
    else:
        tt, bb = _largest_tile(t, tm_target, 16), 1
    nt = t // tt
    tm = bb * tt
    n_tok = b * t
    grid = (n_tok // tm,)

    row3 = lambda i: (i // nt, i % nt, 0)
    mod3 = lambda i: (i // nt, 0, 0)
    row2 = lambda i: (i, 0)
    in_specs = [
        pl.BlockSpec((bb, tt, d), row3),
        pl.BlockSpec((bb, 1, d), mod3),
        pl.BlockSpec((bb, 1, d), mod3),
        pl.BlockSpec((tt, LANES), lambda i: (i % nt, 0)),
        pl.BlockSpec((tt, LANES), lambda i: (i % nt, 0)),
        _const_spec(w["w_lat"].shape),
        _const_spec(w["w_uv_in"].shape),
        _const_spec(w["w_kr"].shape),
        _const_spec((1, q_lora)),
        _const_spec((1, kv_lora)),
        _const_spec((1, d_cmlp)),
        _const_spec((1, d_cmlp)),
        _const_spec(w["w_q"].shape),
    ]
    args = [x, sc, sh, cos_t, sin_t, w["w_lat"], w["w_uv_in"], w["w_kr"], w["q_norm"], w["kv_norm"], w["v_ln_g"],
            w["v_ln_b"], w["w_q"]]
    out_specs = [
        pl.BlockSpec((tm, kv_lora), row2),
        pl.BlockSpec((tm, LANES), row2),
        pl.BlockSpec((tm, d_cmlp), row2),
        pl.BlockSpec((tm, d_cmlp), row2),
    ]
    out_shape = [
        jax.ShapeDtypeStruct((n_tok, kv_lora), F32),
        jax.ShapeDtypeStruct((n_tok, LANES), F32),
        jax.ShapeDtypeStruct((n_tok, d_cmlp), F32),
        jax.ShapeDtypeStruct((n_tok, d_cmlp), F32),
    ]
    if absorbed:
        in_specs.append(_const_spec(w["w_uk_t"].shape))
        args.append(w["w_uk_t"])
        d_q = kv_lora + LANES
        out_specs.append(pl.BlockSpec((bb, N_HEADS, t, d_q), lambda i: (i, 0, 0, 0)))
        out_shape.append(jax.ShapeDtypeStruct((b, N_HEADS, t, d_q), F32))
    else:
        in_specs += [_const_spec(w["w_uk_f"].shape), _const_spec(w["w_uv_f"].shape)]
        args += [w["w_uk_f"], w["w_uv_f"]]
        out_specs += [
            pl.BlockSpec((tm, N_HEADS * D_QK), row2),
            pl.BlockSpec((tm, N_HEADS * D_QK), row2),
            pl.BlockSpec((tm, N_HEADS * D_V), row2),
        ]
        out_shape += [
            jax.ShapeDtypeStruct((n_tok, N_HEADS * D_QK), BF16),
            jax.ShapeDtypeStruct((n_tok, N_HEADS * D_QK), BF16),
            jax.ShapeDtypeStruct((n_tok, N_HEADS * D_V), BF16),
        ]
    kern = functools.partial(_inproj_kernel, absorbed=absorbed, q_lora=q_lora, kv_lora=kv_lora, d_cmlp=d_cmlp)
    return pl.pallas_call(
        kern, grid=grid, in_specs=in_specs, out_specs=out_specs, out_shape=out_shape,
        compiler_params=_params(("arbitrary",)),
        name="inproj_sample" if absorbed else "inproj_prompt",
    )(*args)


def _flash_kernel(q_ref, k_ref, v_ref, o_ref):
    tq = q_ref.shape[0]
    t = k_ref.shape[0]
    qi = pl.program_id(2)
    row = lax.broadcasted_iota(jnp.int32, (tq, tq), 0)
    col = lax.broadcasted_iota(jnp.int32, (tq, tq), 1)

    def one_head(hh, n_past):
        qk = slice(hh * D_QK, (hh + 1) * D_QK)
        vo = slice(hh * D_V, (hh + 1) * D_V)
        diag = pl.ds(n_past, tq)
        q = q_ref[:, qk]
        s_d = jnp.where(col <= row, _dot_nt(q, k_ref[diag, qk]), NEG)
        m = jnp.max(s_d, axis=-1, keepdims=True)
        if n_past:
            s_p = _dot_nt(q, k_ref[0:n_past, qk])
            m = jnp.maximum(m, jnp.max(s_p, axis=-1, keepdims=True))
            p_p = jnp.exp(s_p - m)
        p_d = jnp.exp(s_d - m)
        l = jnp.sum(p_d, axis=-1, keepdims=True)
        o = _dot(p_d.astype(BF16), v_ref[diag, vo])
        if n_past:
            l = l + jnp.sum(p_p, axis=-1, keepdims=True)
            o = o + _dot(p_p.astype(BF16), v_ref[0:n_past, vo])
        o_ref[:, vo] = (o / l).astype(o_ref.dtype)

    for c in range(t // tq):
        @pl.when(qi == c)
        def _(n_past=c * tq):
            for hh in range(FLASH_HEADS):
                one_head(hh, n_past)


FLASH_HEADS = 4


def _flash(q_cat, k_cat, v_h, b, t):
    tq = _largest_tile(t, 512, 16)
    nq = t // tq
    hp = FLASH_HEADS
    return pl.pallas_call(
        _flash_kernel,
        grid=(b, N_HEADS // hp, nq),
        in_specs=[
            pl.BlockSpec((tq, hp * D_QK), lambda bi, h, i: (bi * nq + i, h)),
            pl.BlockSpec((t, hp * D_QK), lambda bi, h, i: (bi, h)),
            pl.BlockSpec((t, hp * D_V), lambda bi, h, i: (bi, h)),
        ],
        out_specs=pl.BlockSpec((tq, hp * D_V), lambda bi, h, i: (bi * nq + i, h)),
        out_shape=jax.ShapeDtypeStruct((b * t, N_HEADS * D_V), BF16),
        compiler_params=_params(("arbitrary", "arbitrary", "arbitrary")),
        name="flash_prompt",
    )(q_cat, k_cat, v_h)


PAGE_RING = 4


FFN_TM = 1024
FFN_TF = 256


def _paged_kernel(pt_ref, q_ref, ckv_ref, kr_ref, lat_hbm, krp_hbm, *refs, layer, n_group, n_steps, kv_lora, ffn):
    if ffn is None:
        o_ref, lat_buf, krp_buf, sem = refs
    else:
        (h_hbm, wg_hbm, wu_hbm, wd_hbm, o_ref, f_hbm, lat_buf, krp_buf, sem,
         wg_buf, wu_buf, wd_buf, h_buf, facc, fsem) = refs
    b = pl.program_id(0)
    nb = pl.num_programs(0)
    s_new = ckv_ref.shape[1]
    n_rows = N_HEADS * s_new

    def page_copies(bi, g):
        slot = g % PAGE_RING
        copies = []
        for k in range(n_group):
            page = pt_ref[bi, g * n_group + k]
            rows = pl.ds(k * PAGE_SIZE, PAGE_SIZE)
            copies.append(pltpu.make_async_copy(lat_hbm.at[layer, page], lat_buf.at[slot, rows, :], sem.at[0, slot]))
            copies.append(pltpu.make_async_copy(krp_hbm.at[layer, page], krp_buf.at[slot, :, rows], sem.at[1, slot]))
        return copies

    def start_group(g):
        if g < n_steps:
            for cp in page_copies(b, g):
                cp.start()
        else:
            @pl.when(b + 1 < nb)
            def _():
                for cp in page_copies(b + 1, g - n_steps):
                    cp.start()

    @pl.when(b == 0)
    def _():
        for g in range(PAGE_RING - 1):
            start_group(g)

    q = q_ref[0].reshape(n_rows, kv_lora + LANES)
    q_lat = q[:, :kv_lora].astype(BF16)
    q_rope = q[:, kv_lora:kv_lora + D_ROPE].astype(BF16)

    def scores(g):
        slot = g % PAGE_RING
        return _dot_nt(q_lat, lat_buf[slot].astype(BF16)) + _dot(q_rope, krp_buf[slot].astype(BF16))

    def attend(state, s, kv_bf):
        m_old, l_old, acc = state
        m_new = jnp.maximum(m_old, jnp.max(s, axis=-1, keepdims=True))
        corr = jnp.exp(m_old - m_new)
        p = jnp.exp(s - m_new)
        l_new = l_old * corr + jnp.sum(p, axis=-1, keepdims=True)
        return m_new, l_new, acc * corr + _dot(p.astype(BF16), kv_bf)

    pad = PAGE_SIZE - s_new
    kv0 = jnp.concatenate([ckv_ref[0], jnp.zeros((pad, kv_lora), F32)], axis=0).astype(BF16)
    kr0 = jnp.concatenate([kr_ref[0][:, :D_ROPE], jnp.zeros((pad, D_ROPE), F32)], axis=0).astype(BF16)
    s0 = _dot_nt(q_lat, kv0) + _dot_nt(q_rope, kr0)
    qpos = lax.broadcasted_iota(jnp.int32, (n_rows, PAGE_SIZE), 0) % s_new
    kpos = lax.broadcasted_iota(jnp.int32, (n_rows, PAGE_SIZE), 1)
    s0 = jnp.where(kpos <= qpos, s0, NEG)
    state = (jnp.full((n_rows, 1), NEG, F32), jnp.zeros((n_rows, 1), F32), jnp.zeros((n_rows, kv_lora), F32))
    state = attend(state, s0, kv0)

    if ffn is not None:
        n_mt, n_ft, n_extra = ffn
        n_q = n_mt * n_ft
        q_first = b + jnp.minimum(b, n_extra)

        def w_copies(q):
            ft, slot = lax.rem(q, n_ft), lax.rem(q, 2)
            return [pltpu.make_async_copy(wg_hbm.at[ft], wg_buf.at[slot], fsem.at[slot]),
                    pltpu.make_async_copy(wu_hbm.at[ft], wu_buf.at[slot], fsem.at[slot]),
                    pltpu.make_async_copy(wd_hbm.at[ft], wd_buf.at[slot], fsem.at[slot])]

        def h_copy(mt):
            slot = lax.rem(mt, 2)
            return pltpu.make_async_copy(h_hbm.at[pl.ds(mt * FFN_TM, FFN_TM), :], h_buf.at[slot], fsem.at[2 + slot])

        def out_copy(mt):
            return pltpu.make_async_copy(facc, f_hbm.at[pl.ds(mt * FFN_TM, FFN_TM), :], fsem.at[4])

        def ffn_quantum(q):
            mt, ft = lax.div(q, n_ft), lax.rem(q, n_ft)
            for cp in w_copies(q):
                cp.wait()

            @pl.when(q + 1 < n_q)
            def _():
                for cp in w_copies(q + 1):
                    cp.start()

            @pl.when(ft == 0)
            def _():
                h_copy(mt).wait()

                @pl.when(mt + 1 < n_mt)
                def _():
                    h_copy(mt + 1).start()

                @pl.when(mt > 0)
                def _():
                    out_copy(mt - 1).wait()

                facc[...] = jnp.zeros(facc.shape, F32)

            wslot = lax.rem(q, 2)
            hq = h_buf[lax.rem(mt, 2)]
            a = (jax.nn.silu(_dot(hq, wg_buf[wslot])) * _dot(hq, wu_buf[wslot])).astype(BF16)
            facc[...] += _dot(a, wd_buf[wslot])

            @pl.when(ft == n_ft - 1)
            def _():
                out_copy(mt).start()

        @pl.when(b == 0)
        def _():
            for cp in w_copies(0):
                cp.start()
            h_copy(0).start()

    for cp in page_copies(b, 0):
        cp.wait()
    s = scores(0)
    for g in range(n_steps):
        start_group(g + PAGE_RING - 1)
        if g + 1 < n_steps:
            for cp in page_copies(b, g + 1):
                cp.wait()
            s_next = scores(g + 1)
        state = attend(state, s, lat_buf[g % PAGE_RING].astype(BF16))
        if g + 1 < n_steps:
            s = s_next
        if ffn is not None and g == n_steps // 2 - 1:
            ffn_quantum(q_first)

    _, l_fin, acc = state
    o_ref[0] = (acc / l_fin).reshape(N_HEADS, s_new, kv_lora)

    if ffn is not None:
        @pl.when(b < n_extra)
        def _():
            ffn_quantum(q_first + 1)

        @pl.when(b == nb - 1)
        def _():
            out_copy(n_mt - 1).wait()


def _ffn_plan(n_tok, n_ft, n_batch):
    if n_tok % FFN_TM:
        return None
    n_q = (n_tok // FFN_TM) * n_ft
    if not n_batch <= n_q <= 2 * n_batch:
        return None
    return n_tok // FFN_TM, n_ft, n_q - n_batch


def _paged(q_cat, ckv_new, kr_new, cache_latent, cache_krope_t, page_table, layer, ffn_args=None):
    b, _, s_new, d_q = q_cat.shape
    kv_lora = ckv_new.shape[-1]
    n_pages = page_table.shape[1]
    assert n_pages % PAGE_RING == 0
    n_group = _largest_tile(n_pages // PAGE_RING, 16, 1)
    n_steps = n_pages // n_group
    in_specs = [
        pl.BlockSpec((1, N_HEADS, s_new, d_q), lambda bi, pt: (bi, 0, 0, 0)),
        pl.BlockSpec((1, s_new, kv_lora), lambda bi, pt: (bi, 0, 0)),
        pl.BlockSpec((1, s_new, LANES), lambda bi, pt: (bi, 0, 0)),
        pl.BlockSpec(memory_space=pl.ANY),
        pl.BlockSpec(memory_space=pl.ANY),
    ]
    args = [q_cat, ckv_new.reshape(b, s_new, kv_lora), kr_new.reshape(b, s_new, LANES), cache_latent, cache_krope_t]
    out_specs = [pl.BlockSpec((1, N_HEADS, s_new, kv_lora), lambda bi, pt: (bi, 0, 0, 0))]
    out_shape = [jax.ShapeDtypeStruct((b, N_HEADS, s_new, kv_lora), F32)]
    scratch = [
        pltpu.VMEM((PAGE_RING, n_group * PAGE_SIZE, kv_lora), F32),
        pltpu.VMEM((PAGE_RING, D_ROPE, n_group * PAGE_SIZE), F32),
        pltpu.SemaphoreType.DMA((2, PAGE_RING)),
    ]
    plan = None
    if ffn_args is not None:
        h2, wg_t, wu_t, wd_t, plan = ffn_args
        d = h2.shape[1]
        in_specs += [pl.BlockSpec(memory_space=pl.ANY)] * 4
        args += [h2, wg_t, wu_t, wd_t]
        out_specs.append(pl.BlockSpec(memory_space=pl.ANY))
        out_shape.append(jax.ShapeDtypeStruct(h2.shape, F32))
        scratch += [
            pltpu.VMEM((2, d, FFN_TF), BF16),
            pltpu.VMEM((2, d, FFN_TF), BF16),
            pltpu.VMEM((2, FFN_TF, d), BF16),
            pltpu.VMEM((2, FFN_TM, d), BF16),
            pltpu.VMEM((FFN_TM, d), F32),
            pltpu.SemaphoreType.DMA((5,)),
        ]
    grid_spec = pltpu.PrefetchScalarGridSpec(
        num_scalar_prefetch=1, grid=(b,), in_specs=in_specs, out_specs=out_specs, scratch_shapes=scratch)
    kern = functools.partial(_paged_kernel, layer=layer, n_group=n_group, n_steps=n_steps, kv_lora=kv_lora, ffn=plan)
    return pl.pallas_call(
        kern, grid_spec=grid_spec, out_shape=out_shape,
        compiler_params=_params(("arbitrary",)),
        name="paged_ffn" if plan else "paged_sample",
    )(page_table, *args)


def _mix_kernel(*refs, absorbed, alpha, run, d_att):
    if absorbed:
        (x_ref, g1_ref, sc2_ref, sh2_ref, att_ref, wuv_ref, u_ref, v_ref, ws_ref, bs_ref, wout_ref, lng_ref, lnb_ref,
         x1_out, h2_out, cm_scr, att_scr) = refs
    else:
        (x_ref, g1_ref, sc2_ref, sh2_ref, att_ref, u_ref, v_ref, ws_ref, bs_ref, wout_ref, lng_ref, lnb_ref,
         x1_out, h2_out, cm_scr) = refs
    bb, tt, d = x_ref.shape
    tm = bb * tt
    mb = bs_ref.shape[0]
    d_cmlp = u_ref.shape[1]
    d_head = d_cmlp // N_CMLP_HEADS

    if absorbed:
        for hd in range(N_HEADS):
            o_lat = att_ref[:, hd].reshape(tm, att_ref.shape[-1]).astype(BF16)
            att_scr[:, hd * D_V:(hd + 1) * D_V] = _dot(o_lat, wuv_ref[hd]).astype(BF16)
        mix = _dot(att_scr[...], wout_ref[0:d_att, :])
    else:
        mix = _dot(att_ref[...], wout_ref[0:d_att, :])

    r = lax.broadcasted_iota(jnp.int32, (mb, mb), 0)
    c = lax.broadcasted_iota(jnp.int32, (mb, mb), 1)
    shift = run.bit_length() - 1
    keep = ((r >> shift) == (c >> shift)) & ((c & (run - 1)) <= (r & (run - 1)))
    bs = bs_ref[...]
    for hd in range(N_CMLP_HEADS):
        w_rows = ws_ref[hd]
        w_blk = jnp.concatenate([w_rows] * (mb // w_rows.shape[0]), axis=0)
        w_blk = jnp.concatenate([w_blk] * (mb // w_rows.shape[1]), axis=1)
        wm = jnp.where(keep, w_blk, 0.0).astype(BF16)
        bias = bs[:, hd:hd + 1]
        cols = slice(hd * d_head, (hd + 1) * d_head)
        for blk in range(tm // mb):
            rows = slice(blk * mb, (blk + 1) * mb)
            mixed = _dot(wm, v_ref[rows, cols].astype(BF16)) + bias
            cm_scr[rows, cols] = (u_ref[rows, cols] * mixed).astype(BF16)

    mix = mix + _dot(cm_scr[...], wout_ref[d_att:d_att + d_cmlp, :])
    y = alpha * x_ref[...] + g1_ref[...] * mix.reshape(bb, tt, d)
    x1 = _layernorm(y, lng_ref[...], lnb_ref[...])
    x1_out[...] = x1
    h2 = x1 * (1.0 + sc2_ref[...]) + sh2_ref[...]
    h2_out[...] = h2.reshape(tm, d).astype(BF16)


def _mix(x, g1, sc2, sh2, att, u_act, v_ln, w, *, absorbed, alpha, tm_target):
    b, t, d = x.shape
    d_cmlp = u_act.shape[1]
    d_att = N_HEADS * D_V
    run = min(t, CHUNK)
    assert run & (run - 1) == 0 and t % run == 0
    if absorbed:
        tt, bb = t, _largest_tile(b, max(1, tm_target // t), 1)
    else:
        tt, bb = _largest_tile(t, tm_target, CHUNK), 1
    nt = t // tt
    tm = bb * tt
    mb = w["bs_rep"].shape[0]
    assert tm % mb == 0
    n_tok = b * t
    row3 = lambda i: (i // nt, i % nt, 0)
    mod3 = lambda i: (i // nt, 0, 0)
    row2 = lambda i: (i, 0)
    in_specs = [
        pl.BlockSpec((bb, tt, d), row3),
        pl.BlockSpec((bb, 1, d), mod3),
        pl.BlockSpec((bb, 1, d), mod3),
        pl.BlockSpec((bb, 1, d), mod3),
    ]
    args = [x, g1, sc2, sh2]
    if absorbed:
        in_specs += [pl.BlockSpec((bb, N_HEADS, t, att.shape[-1]), lambda i: (i, 0, 0, 0)),
                     _const_spec(w["w_uv_h"].shape)]
        args += [att, w["w_uv_h"]]
    else:
        in_specs.append(pl.BlockSpec((tm, d_att), row2))
        args.append(att)
    in_specs += [
        pl.BlockSpec((tm, d_cmlp), row2),
        pl.BlockSpec((tm, d_cmlp), row2),
        _const_spec(w["ws_rep"].shape),
        _const_spec(w["bs_rep"].shape),
        _const_spec(w["w_out"].shape),
        _const_spec((1, d)),
        _const_spec((1, d)),
    ]
    args += [u_act, v_ln, w["ws_rep"], w["bs_rep"], w["w_out"], w["ln1_g"], w["ln1_b"]]
    kern = functools.partial(_mix_kernel, absorbed=absorbed, alpha=alpha, run=run, d_att=d_att)
    return pl.pallas_call(
        kern, grid=(n_tok // tm,), in_specs=in_specs,
        out_specs=[pl.BlockSpec((bb, tt, d), row3), pl.BlockSpec((tm, d), row2)],
        out_shape=[jax.ShapeDtypeStruct((b, t, d), F32), jax.ShapeDtypeStruct((n_tok, d), BF16)],
        scratch_shapes=[pltpu.VMEM((tm, d_cmlp), BF16)] + ([pltpu.VMEM((tm, d_att), BF16)] if absorbed else []),
        compiler_params=_params(("arbitrary",)),
        name="mix_sample" if absorbed else "mix_prompt",
    )(*args)


def _ffn_kernel(h_ref, x1_ref, g2_ref, wg_ref, wu_ref, wd_ref, lng_ref, lnb_ref, o_ref, *, alpha):
    f = pl.program_id(1)
    bb, tt, d = x1_ref.shape

    @pl.when(f == 0)
    def _():
        o_ref[...] = jnp.zeros(o_ref.shape, F32)

    h = h_ref[...]
    a = (jax.nn.silu(_dot(h, wg_ref[...])) * _dot(h, wu_ref[...])).astype(BF16)
    o_ref[...] += _dot(a, wd_ref[...]).reshape(bb, tt, d)

    @pl.when(f == pl.num_programs(1) - 1)
    def _():
        y = alpha * x1_ref[...] + g2_ref[...] * o_ref[...]
        o_ref[...] = _layernorm(y, lng_ref[...], lnb_ref[...])


def _row_tiles(b, t, absorbed, tm_target):
    if absorbed:
        return _largest_tile(b, max(1, tm_target // t), 1), t
    return 1, _largest_tile(t, tm_target, 16)


def _ffn(h2, x1, g2, w, *, absorbed, alpha, tm_target):
    b, t, d = x1.shape
    n_ft = w["w_gate_t"].shape[0]
    bb, tt = _row_tiles(b, t, absorbed, tm_target)
    nt = t // tt
    tm = bb * tt
    n_row_tiles = b * t // tm
    row3 = lambda i, f: (i // nt, i % nt, 0)
    w_col = pl.BlockSpec((None, d, FFN_TF), lambda i, f: (f, 0, 0))
    w_row = pl.BlockSpec((None, FFN_TF, d), lambda i, f: (f, 0, 0))
    once = {"pipeline_mode": pl.Buffered(1)} if n_row_tiles == 1 else {}
    return pl.pallas_call(
        functools.partial(_ffn_kernel, alpha=alpha),
        grid=(n_row_tiles, n_ft),
        in_specs=[
            pl.BlockSpec((tm, d), lambda i, f: (i, 0), **once),
            pl.BlockSpec((bb, tt, d), row3, **once),
            pl.BlockSpec((bb, 1, d), lambda i, f: (i // nt, 0, 0)),
            w_col, w_col, w_row,
            _const_spec((1, d)),
            _const_spec((1, d)),
        ],
        out_specs=pl.BlockSpec((bb, tt, d), row3),
        out_shape=jax.ShapeDtypeStruct((b, t, d), F32),
        compiler_params=_params(("arbitrary", "arbitrary")),
        name="ffn_sample" if absorbed else "ffn_prompt",
    )(h2, x1, g2, w["w_gate_t"], w["w_up_t"], w["w_down_t"], w["ln2_g"], w["ln2_b"])


def _resid_ln_kernel(f_ref, x1_ref, g2_ref, lng_ref, lnb_ref, o_ref, *, alpha):
    bb, tt, d = x1_ref.shape
    y = alpha * x1_ref[...] + g2_ref[...] * f_ref[...].reshape(bb, tt, d)
    o_ref[...] = _layernorm(y, lng_ref[...], lnb_ref[...])


def _resid_ln(f, x1, g2, w, *, absorbed, alpha, tm_target):
    b, t, d = x1.shape
    bb, tt = _row_tiles(b, t, absorbed, tm_target)
    nt = t // tt
    tm = bb * tt
    row3 = lambda i: (i // nt, i % nt, 0)
    return pl.pallas_call(
        functools.partial(_resid_ln_kernel, alpha=alpha),
        grid=(b * t // tm,),
        in_specs=[
            pl.BlockSpec((tm, d), lambda i: (i, 0)),
            pl.BlockSpec((bb, tt, d), row3),
            pl.BlockSpec((bb, 1, d), lambda i: (i // nt, 0, 0)),
            _const_spec((1, d)),
            _const_spec((1, d)),
        ],
        out_specs=pl.BlockSpec((bb, tt, d), row3),
        out_shape=jax.ShapeDtypeStruct((b, t, d), F32),
        compiler_params=_params(("arbitrary",)),
        name="resid_ln",
    )(f, x1, g2, w["ln2_g"], w["ln2_b"])


def _rope_pair(a):
    half = D_ROPE // 2
    return jnp.concatenate([a, -a[..., half:], a[..., :half]], axis=-1)


def _layer_weights(l, w_in, q_norm, w_uq, kv_norm, w_uk, w_uv, v_ln_g, v_ln_b, w_s, b_s, w_out, ln1_g, ln1_b,
                   w_gate, w_up, w_down, ln2_g, ln2_b, t_prompt, t_sample, mb_sample):
    q_lora, kv_lora, d_cmlp = q_norm.shape[-1], kv_norm.shape[-1], v_ln_g.shape[-1]
    wi = w_in[l]
    i1 = q_lora + kv_lora
    i2 = i1 + D_ROPE
    wq = w_uq[l].reshape(q_lora, N_HEADS, D_NOPE + D_ROPE)
    w_q = jnp.concatenate([wq[..., :D_NOPE].reshape(q_lora, -1),
                           _rope_pair(wq[..., D_NOPE:]).reshape(q_lora, -1)], axis=1).astype(BF16)
    row = lambda a: a[l].reshape(1, -1)

    def mixing(t, mb):
        run = min(t, CHUNK)
        return (jnp.tile(w_s[l][:, :run, :run], (1, 1, max(1, min(LANES, mb) // run))),
                jnp.tile(b_s[l][:, :run], (1, mb // run)).T)

    def col_tiles(a):
        return jnp.transpose(a.astype(BF16).reshape(a.shape[0], -1, FFN_TF), (1, 0, 2))

    ws_p, bs_p = mixing(t_prompt, min(t_prompt, CHUNK))
    ws_s, bs_s = mixing(t_sample, mb_sample)
    common = {
        "w_lat": wi[:, :i1].astype(BF16), "w_uv_in": wi[:, i2:].astype(BF16),
        "w_kr": _rope_pair(wi[:, i1:i2]).astype(BF16),
        "w_q": w_q, "q_norm": row(q_norm), "kv_norm": row(kv_norm),
        "v_ln_g": row(v_ln_g), "v_ln_b": row(v_ln_b), "w_out": w_out[l].astype(BF16),
        "ln1_g": row(ln1_g), "ln1_b": row(ln1_b), "ln2_g": row(ln2_g), "ln2_b": row(ln2_b),
        "w_gate_t": col_tiles(w_gate[l]), "w_up_t": col_tiles(w_up[l]),
        "w_down_t": w_down[l].astype(BF16).reshape(-1, FFN_TF, w_down.shape[-1]),
    }
    prompt = dict(common, w_uk_f=w_uk[l].reshape(kv_lora, -1).astype(BF16),
                  w_uv_f=w_uv[l].reshape(kv_lora, -1).astype(BF16), ws_rep=ws_p, bs_rep=bs_p)
    sample = dict(common, w_uk_t=jnp.transpose(w_uk[l], (1, 2, 0)).astype(BF16),
                  w_uv_h=jnp.transpose(w_uv[l], (1, 0, 2)).astype(BF16), ws_rep=ws_s, bs_rep=bs_s)
    return prompt, sample


def _rope_tables(pos):
    inv = ROPE_THETA ** (-jnp.arange(0, D_ROPE, 2, dtype=F32) / D_ROPE)
    ang = pos[:, None] * inv[None, :]
    z = jnp.zeros((pos.shape[0], D_ROPE), F32)
    cos, sin = jnp.cos(ang), jnp.sin(ang)
    return jnp.concatenate([cos, cos, z], axis=1), jnp.concatenate([sin, sin, z], axis=1)


def _mods(mod):
    return [m[:, None, :] for m in jnp.split(mod, 6, axis=-1)]


def kernel(x_prompt, x_sample, cache_latent, cache_krope, page_table, c_prompt, c_sample, w_ada, b_ada, w_in, q_norm,
           w_uq, kv_norm, w_uk, w_uv, v_ln_g, v_ln_b, w_s, b_s, w_out, ln1_g, ln1_b, w_gate, w_up, w_down, ln2_g,
           ln2_b):
    depth = w_in.shape[0]
    alpha = (2.0 * depth) ** 0.25
    bp, tp, d = x_prompt.shape
    bs, ts, _ = x_sample.shape
    past_len = page_table.shape[1] * PAGE_SIZE
    kv_lora = kv_norm.shape[-1]
    d_cmlp = v_ln_g.shape[-1]
    cdt = cache_latent.dtype

    xp = x_prompt.astype(F32)
    xs = x_sample.astype(F32)
    cos_p, sin_p = _rope_tables(jnp.arange(tp, dtype=F32))
    cos_s, sin_s = _rope_tables(past_len + jnp.arange(ts, dtype=F32))
    c_all = jnp.concatenate([c_prompt, c_sample], axis=0).astype(F32)
    krope_t = jnp.swapaxes(cache_krope, 2, 3)
    tm_s = _largest_tile(bs, max(1, 256 // ts), 1) * ts

    lat_p, kr_p, v_p, lat_s, kr_s, v_s = [], [], [], [], [], []
    for l in range(depth):
        wp, ws = _layer_weights(l, w_in, q_norm, w_uq, kv_norm, w_uk, w_uv, v_ln_g, v_ln_b, w_s, b_s, w_out, ln1_g,
                                ln1_b, w_gate, w_up, w_down, ln2_g, ln2_b, tp, ts, tm_s)
        mod = _ada(c_all, w_ada[l], b_ada[l])
        sh1, sc1, g1, sh2, sc2, g2 = _mods(mod[:bp])
        ckv, kr, u_act, v_ln, q_cat, k_cat, v_h = _inproj(xp, sc1, sh1, cos_p, sin_p, wp, absorbed=False,
                                                          tm_target=256)
        att = _flash(q_cat, k_cat, v_h, bp, tp)
        x1_p, h2_p = _mix(xp, g1, sc2, sh2, att, u_act, v_ln, wp, absorbed=False, alpha=alpha, tm_target=512)
        g2_p = g2
        lat_p.append(ckv.reshape(bp, tp, kv_lora))
        kr_p.append(kr[:, :D_ROPE].reshape(bp, tp, D_ROPE))
        v_p.append(v_ln.reshape(bp, tp, d_cmlp)[:, tp - min(tp, CHUNK):])

        sh1, sc1, g1, sh2, sc2, g2 = _mods(mod[bp:])
        ckv, kr, u_act, v_ln, q_cat = _inproj(xs, sc1, sh1, cos_s, sin_s, ws, absorbed=True, tm_target=tm_s)
        plan = _ffn_plan(bp * tp, wp["w_gate_t"].shape[0], bs)
        if plan is None:
            o_lat, = _paged(q_cat, ckv, kr, cache_latent, krope_t, page_table, l)
            xp = _ffn(h2_p, x1_p, g2_p, wp, absorbed=False, alpha=alpha, tm_target=512)
        else:
            o_lat, f_p = _paged(q_cat, ckv, kr, cache_latent, krope_t, page_table, l,
                                (h2_p, wp["w_gate_t"], wp["w_up_t"], wp["w_down_t"], plan))
            xp = _resid_ln(f_p, x1_p, g2_p, wp, absorbed=False, alpha=alpha, tm_target=512)
        x1, h2 = _mix(xs, g1, sc2, sh2, o_lat, u_act, v_ln, ws, absorbed=True, alpha=alpha, tm_target=tm_s)
        xs = _ffn(h2, x1, g2, ws, absorbed=True, alpha=alpha, tm_target=1024)
        lat_s.append(ckv.reshape(bs, ts, kv_lora))
        kr_s.append(kr[:, :D_ROPE].reshape(bs, ts, D_ROPE))
        v_s.append(v_ln.reshape(bs, ts, d_cmlp))

    return (xp.astype(x_prompt.dtype), xs.astype(x_sample.dtype),
            jnp.stack(lat_p).astype(cdt), jnp.stack(kr_p).astype(cdt), jnp.stack(v_p).astype(cdt),
            jnp.stack(lat_s).astype(cdt), jnp.stack(kr_s).astype(cdt), jnp.stack(v_s).astype(cdt))
```
